```python
import jax
import jax.numpy as jnp
from jax import lax
import numpy as np

D_MODEL = 2048
BATCH = 2
SEQ = 4096
DEPTH = 1
DEC_BATCH = 8
DEC_SEQ = 8
PAST_LEN = 16384
PAGE_SIZE = 128

SB_HEADS = 16
SB_HEAD_DIM = 128
SB_BLOCK = 128
SB_W = SB_HEADS * SB_HEAD_DIM
SB_BIAS_INIT = -8.0
RET_HEADS = 8
RET_DK = 128
RET_DV = 256
RET_CHUNK = 128
RET_QK_W = RET_HEADS * RET_DK
RET_V_W = RET_HEADS * RET_DV
ROPE_BASE = 10000.0
N_GROUPS = 4
EXPERTS_PER_GROUP = 8
N_EXPERTS = N_GROUPS * EXPERTS_PER_GROUP
EXPERT_FF = 256
TOP_K_IN_GROUP = 2
LN_EPS = 1e-5
GN_EPS = 1e-5
DN_ALPHA = (2.0 * DEPTH) ** 0.25
DN_BETA = (8.0 * DEPTH) ** -0.25

kernel_name = 'stick_breaking_retention_hmoe_step'


def _col_widths():
    return (SB_W, SB_W, SB_W, RET_QK_W, RET_QK_W, RET_V_W, RET_V_W, D_MODEL, D_MODEL)


def _layer_norm(x, g, b):
    xf = x.astype(jnp.float32)
    mu = jnp.mean(xf, axis=-1, keepdims=True)
    var = jnp.mean(jnp.square(xf - mu), axis=-1, keepdims=True)
    return ((xf - mu) * lax.rsqrt(var + LN_EPS) * g + b).astype(x.dtype)


def _rope(x, pos):
    half = x.shape[-1] // 2
    inv = ROPE_BASE ** (-jnp.arange(half, dtype=jnp.float32) / half)
    ang = pos.astype(jnp.float32)[:, None] * inv[None, :]
    c = jnp.cos(ang)[None, :, None, :]
    s = jnp.sin(ang)[None, :, None, :]
    xf = x.astype(jnp.float32)
    x1, x2 = xf[..., :half], xf[..., half:]
    return jnp.concatenate([x1 * c - x2 * s, x1 * s + x2 * c], axis=-1)


def _project(x, w_in):
    b, l, _ = x.shape
    p = jnp.einsum('bld,dc->blc', x, w_in)
    offs = []
    acc = 0
    for w in _col_widths()[:-1]:
        acc += w
        offs.append(acc)
    sq, sk, sv, rq, rk, rv, rg, ga, gb = jnp.split(p, offs, axis=-1)
    sq = sq.reshape(b, l, SB_HEADS, SB_HEAD_DIM)
    sk = sk.reshape(b, l, SB_HEADS, SB_HEAD_DIM)
    sv = sv.reshape(b, l, SB_HEADS, SB_HEAD_DIM)
    rq = rq.reshape(b, l, RET_HEADS, RET_DK)
    rk = rk.reshape(b, l, RET_HEADS, RET_DK)
    rv = rv.reshape(b, l, RET_HEADS, RET_DV)
    return sq, sk, sv, rq, rk, rv, rg, ga, gb


def _sb_block(q_blk, q_pos, k, v, k_pos, sb_bias):
    z = jnp.einsum('bhqd,bhsd->bhqs', q_blk, k).astype(jnp.float32) * (SB_HEAD_DIM ** -0.5)
    z = z + sb_bias.astype(jnp.float32)[None, :, None, None]
    valid = k_pos[None, :] < q_pos[:, None]
    log_keep = jnp.where(valid, jax.nn.log_sigmoid(-z), 0.0)
    tail = lax.cumsum(log_keep, axis=3, reverse=True) - log_keep
    weights = jnp.where(valid, jnp.exp(jax.nn.log_sigmoid(z) + tail), 0.0)
    return jnp.einsum('bhqs,bhsd->bhqd', weights.astype(v.dtype), v)


def _sb_prompt(q, k, v, sb_bias):
    b, s, h, d = q.shape
    blk = min(SB_BLOCK, s)
    nb = s // blk
    qh = q.transpose(0, 2, 1, 3)
    kh = k.transpose(0, 2, 1, 3)
    vh = v.transpose(0, 2, 1, 3)
    pos = jnp.arange(s)
    q_blocks = qh.reshape(b, h, nb, blk, d).transpose(2, 0, 1, 3, 4)
    p_blocks = pos.reshape(nb, blk)
    out = lax.map(lambda a: _sb_block(a[0], a[1], kh, vh, pos, sb_bias), (q_blocks, p_blocks))
    return out.transpose(1, 0, 3, 2, 4).reshape(b, s, h * d)


def _retention(q, k, v, s0, chunk):
    b, l, h, _ = q.shape
    dv = v.shape[-1]
    n = l // chunk

    def blocks(a):
        return a.astype(jnp.float32).reshape(b, n, chunk, h, a.shape[-1]).transpose(1, 0, 3, 2, 4)

    log_g = jnp.log(1.0 - 2.0 ** (-5.0 - jnp.arange(h, dtype=jnp.float32)))
    i = jnp.arange(chunk, dtype=jnp.float32)
    diff = i[:, None] - i[None, :]
    decay = jnp.where(diff >= 0, jnp.exp(jnp.maximum(diff, 0.0)[None] * log_g[:, None, None]), 0.0)
    q_dec = jnp.exp((i + 1.0)[None, :] * log_g[:, None])
    k_dec = jnp.exp((chunk - 1.0 - i)[None, :] * log_g[:, None])
    c_dec = jnp.exp(chunk * log_g)

    def step(state, inp):
        qc, kc, vc = inp
        scores = jnp.einsum('bhid,bhjd->bhij', qc, kc) * decay
        intra = jnp.einsum('bhij,bhjv->bhiv', scores, vc)
        inter = jnp.einsum('bhid,bhdv->bhiv', qc * q_dec[..., None], state)
        new_state = state * c_dec[:, None, None] + jnp.einsum('bhjd,bhjv->bhdv', kc * k_dec[..., None], vc)
        return new_state, intra + inter

    s_fin, o = lax.scan(step, s0.astype(jnp.float32), (blocks(q), blocks(k), blocks(v)))
    o = o.transpose(1, 0, 3, 2, 4).reshape(b, l, h, dv)
    return o, s_fin


def _merge(o_sb, o_ret, rg, ga, gb, gn_g, w_branch_a, w_branch_b, w_out):
    b, l, _ = o_sb.shape
    mu = jnp.mean(o_ret, axis=-1, keepdims=True)
    var = jnp.mean(jnp.square(o_ret - mu), axis=-1, keepdims=True)
    normed = ((o_ret - mu) * lax.rsqrt(var + GN_EPS)).reshape(b, l, RET_V_W) * gn_g
    ret = (normed * jax.nn.silu(rg.astype(jnp.float32))).astype(o_sb.dtype)
    pa = jnp.einsum('blc,cd->bld', o_sb, w_branch_a)
    pb = jnp.einsum('blc,cd->bld', ret, w_branch_b)
    m = jax.nn.sigmoid(ga) * pa + jax.nn.sigmoid(gb) * pb
    return jnp.einsum('bld,de->ble', m, w_out)


def _hier_moe(h, wg, bg, we, be, w1, w3, w2):
    shp = h.shape
    t = h.reshape(-1, shp[-1])
    g_logits = (jnp.einsum('td,dg->tg', t, wg) + bg).astype(jnp.float32)
    g_prob = jax.nn.softmax(g_logits, axis=-1)
    g_onehot = jax.nn.one_hot(jnp.argmax(g_logits, axis=-1), N_GROUPS, dtype=jnp.float32)
    g_weight = jnp.sum(g_prob * g_onehot, axis=-1)
    e_logits = (jnp.einsum('td,dge->tge', t, we) + be).astype(jnp.float32)
    e_sel = jnp.einsum('tge,tg->te', e_logits, g_onehot)
    vals, idx = lax.top_k(e_sel, TOP_K_IN_GROUP)
    wk = jax.nn.softmax(vals, axis=-1)
    within = jnp.einsum('tk,tke->te', wk, jax.nn.one_hot(idx, EXPERTS_PER_GROUP, dtype=jnp.float32))
    gate = (g_weight[:, None, None] * g_onehot[:, :, None] * within[:, None, :]).reshape(-1, N_EXPERTS)
    a = jnp.einsum('td,edf->tef', t, w1)
    c = jnp.einsum('td,edf->tef', t, w3)
    hid = jax.nn.silu(a) * c * gate[:, :, None].astype(t.dtype)
    out = jnp.einsum('tef,efd->td', hid, w2)
    return out.reshape(shp)


def _post_norm_tail(x, mix, ln1_g, ln1_b, ln2_g, ln2_b, wg, bg, we, be, w1, w3, w2):
    h = _layer_norm(DN_ALPHA * x + mix, ln1_g, ln1_b)
    return _layer_norm(DN_ALPHA * h + _hier_moe(h, wg, bg, we, be, w1, w3, w2), ln2_g, ln2_b)


def _layer(xp, xs, ck, cv, sr, page_table, w_in, sb_bias, w_branch_a, w_branch_b, w_out, gn_g,
           ln1_g, ln1_b, ln2_g, ln2_b, wg, bg, we, be, w1, w3, w2):
    b, s, _ = xp.shape
    pos_p = jnp.arange(s)
    sq, sk, sv, rq, rk, rv, rg, ga, gb = _project(xp, w_in)
    o_sb = _sb_prompt(sq, sk, sv, sb_bias)
    s0 = jnp.zeros((b, RET_HEADS, RET_DK, RET_DV), jnp.float32)
    o_ret, st_p = _retention(_rope(rq, pos_p), _rope(rk, pos_p) * (RET_DK ** -0.5), rv, s0,
                             min(RET_CHUNK, s))
    mix_p = _merge(o_sb, o_ret, rg, ga, gb, gn_g, w_branch_a, w_branch_b, w_out)
    yp = _post_norm_tail(xp, mix_p, ln1_g, ln1_b, ln2_g, ln2_b, wg, bg, we, be, w1, w3, w2)

    db, tn, _ = xs.shape
    past = page_table.shape[1] * ck.shape[1]
    sq_s, sk_s, sv_s, rq_s, rk_s, rv_s, rg_s, ga_s, gb_s = _project(xs, w_in)
    k_past = ck[page_table].reshape(db, past, SB_HEADS, SB_HEAD_DIM)
    v_past = cv[page_table].reshape(db, past, SB_HEADS, SB_HEAD_DIM)
    k_all = jnp.concatenate([k_past, sk_s.astype(k_past.dtype)], axis=1).transpose(0, 2, 1, 3)
    v_all = jnp.concatenate([v_past, sv_s.astype(v_past.dtype)], axis=1).transpose(0, 2, 1, 3)
    pos_q = past + jnp.arange(tn)
    pos_k = jnp.arange(past + tn)
    o_sb_s = _sb_block(sq_s.transpose(0, 2, 1, 3).astype(k_all.dtype), pos_q, k_all, v_all, pos_k, sb_bias)
    o_sb_s = o_sb_s.transpose(0, 2, 1, 3).reshape(db, tn, SB_W).astype(xs.dtype)
    chunk_s = tn if tn <= RET_CHUNK else RET_CHUNK
    o_ret_s, st_s = _retention(_rope(rq_s, pos_q), _rope(rk_s, pos_q) * (RET_DK ** -0.5), rv_s, sr,
                               chunk_s)
    mix_s = _merge(o_sb_s, o_ret_s, rg_s, ga_s, gb_s, gn_g, w_branch_a, w_branch_b, w_out)
    ys = _post_norm_tail(xs, mix_s, ln1_g, ln1_b, ln2_g, ln2_b, wg, bg, we, be, w1, w3, w2)
    return yp, ys, sk, sv, st_p.astype(sr.dtype), sk_s, sv_s, st_s.astype(sr.dtype)


def setup_inputs(seed: int = 0) -> dict:
    key = jax.random.key(seed)
    ks = jax.random.split(key, 24)
    f32 = jnp.float32
    d = D_MODEL
    n_pages = PAST_LEN // PAGE_SIZE
    n_used = DEC_BATCH * n_pages
    n_pool = n_used + max(1, n_used // 4)
    widths = _col_widths()
    scales = (1.0, 1.0, DN_BETA, 1.0, 1.0, DN_BETA, 1.0, 1.0, 1.0)
    col_scale = jnp.concatenate([jnp.full((w,), sc * d ** -0.5, f32) for w, sc in zip(widths, scales)])
    nrm = jax.random.normal
    x_prompt = nrm(ks[0], (BATCH, SEQ, d), f32)
    x_sample = nrm(ks[1], (DEC_BATCH, DEC_SEQ, d), f32)
    cache_k = nrm(ks[2], (DEPTH, n_pool, PAGE_SIZE, SB_HEADS, SB_HEAD_DIM), f32)
    cache_v = nrm(ks[3], (DEPTH, n_pool, PAGE_SIZE, SB_HEADS, SB_HEAD_DIM), f32)
    state_ret = nrm(ks[4], (DEPTH, DEC_BATCH, RET_HEADS, RET_DK, RET_DV), f32)
    page_table = jax.random.permutation(ks[5], n_pool)[:n_used].reshape(DEC_BATCH, n_pages).astype(jnp.int32)
    w_in = nrm(ks[6], (DEPTH, d, sum(widths)), f32) * col_scale
    sb_bias = SB_BIAS_INIT + 0.1 * nrm(ks[22], (DEPTH, SB_HEADS), f32)
    w_branch_a = nrm(ks[7], (DEPTH, SB_W, d), f32) * (SB_W ** -0.5) * DN_BETA
    w_branch_b = nrm(ks[8], (DEPTH, RET_V_W, d), f32) * (RET_V_W ** -0.5) * DN_BETA
    w_out = nrm(ks[9], (DEPTH, d, d), f32) * (d ** -0.5) * DN_BETA
    gn_g = 1.0 + 0.01 * nrm(ks[10], (DEPTH, RET_V_W), f32)
    ln1_g = 1.0 + 0.01 * nrm(ks[11], (DEPTH, d), f32)
    ln1_b = 0.01 * nrm(ks[12], (DEPTH, d), f32)
    ln2_g = 1.0 + 0.01 * nrm(ks[13], (DEPTH, d), f32)
    ln2_b = 0.01 * nrm(ks[14], (DEPTH, d), f32)
    w_group_router = nrm(ks[15], (DEPTH, d, N_GROUPS), f32) * (d ** -0.5)
    b_group_router = 0.01 * nrm(ks[16], (DEPTH, N_GROUPS), f32)
    w_expert_router = nrm(ks[17], (DEPTH, d, N_GROUPS, EXPERTS_PER_GROUP), f32) * (d ** -0.5)
    b_expert_router = 0.01 * nrm(ks[18], (DEPTH, N_GROUPS, EXPERTS_PER_GROUP), f32)
    w1 = nrm(ks[19], (DEPTH, N_EXPERTS, d, EXPERT_FF), f32) * (d ** -0.5)
    w3 = nrm(ks[20], (DEPTH, N_EXPERTS, d, EXPERT_FF), f32) * (d ** -0.5)
    w2 = nrm(ks[21], (DEPTH, N_EXPERTS, EXPERT_FF, d), f32) * (EXPERT_FF ** -0.5) * DN_BETA
    return {'x_prompt': x_prompt, 'x_sample': x_sample, 'cache_k': cache_k, 'cache_v': cache_v,
            'state_ret': state_ret, 'page_table': page_table, 'w_in': w_in, 'sb_bias': sb_bias,
            'w_branch_a': w_branch_a, 'w_branch_b': w_branch_b, 'w_out': w_out, 'gn_g': gn_g,
            'ln1_g': ln1_g, 'ln1_b': ln1_b, 'ln2_g': ln2_g, 'ln2_b': ln2_b,
            'w_group_router': w_group_router, 'b_group_router': b_group_router,
            'w_expert_router': w_expert_router, 'b_expert_router': b_expert_router,
            'w1': w1, 'w3': w3, 'w2': w2}


def reference(x_prompt, x_sample, cache_k, cache_v, state_ret, page_table, w_in, sb_bias, w_branch_a,
              w_branch_b, w_out, gn_g, ln1_g, ln1_b, ln2_g, ln2_b, w_group_router, b_group_router,
              w_expert_router, b_expert_router, w1, w3, w2):
    xp, xs = x_prompt, x_sample
    kp_l, vp_l, sp_l, ks_l, vs_l, ss_l = [], [], [], [], [], []
    for layer in range(DEPTH):
        xp, xs, kp, vp, sp, k_s, v_s, s_s = _layer(
            xp, xs, cache_k[layer], cache_v[layer], state_ret[layer], page_table,
            w_in[layer], sb_bias[layer], w_branch_a[layer], w_branch_b[layer], w_out[layer], gn_g[layer],
            ln1_g[layer], ln1_b[layer], ln2_g[layer], ln2_b[layer],
            w_group_router[layer], b_group_router[layer], w_expert_router[layer],
            b_expert_router[layer], w1[layer], w3[layer], w2[layer])
        kp_l.append(kp)
        vp_l.append(vp)
        sp_l.append(sp)
        ks_l.append(k_s)
        vs_l.append(v_s)
        ss_l.append(s_s)
    k_prompt = jnp.stack(kp_l)
    v_prompt = jnp.stack(vp_l)
    ret_state_prompt = jnp.stack(sp_l)
    k_sample = jnp.stack(ks_l)
    v_sample = jnp.stack(vs_l)
    ret_state_sample = jnp.stack(ss_l)
    return (xp, xs, k_prompt, v_prompt, ret_state_prompt, k_sample, v_sample, ret_state_sample)
```

```python
import functools

import numpy as np
import jax
import jax.numpy as jnp
from jax import lax
from jax.experimental import pallas as pl
from jax.experimental.pallas import tpu as pltpu

F32 = jnp.float32
BF16 = jnp.bfloat16

SB_HEADS = 16
SB_HEAD_DIM = 128
RET_HEADS = 8
RET_DK = 128
RET_DV = 256
RET_CHUNK = 128
ROPE_BASE = 10000.0
N_GROUPS = 4
EXPERTS_PER_GROUP = 8
N_EXPERTS = N_GROUPS * EXPERTS_PER_GROUP
LN_EPS = 1e-5
GN_EPS = 1e-5

V7X_VMEM_LIMIT_BYTES = 56 * 1024 * 1024
LANES = 128
MOE_TILE = 256
ROUTE_LANES = LANES


def _cparams(sem):
    return pltpu.CompilerParams(dimension_semantics=sem, vmem_limit_bytes=V7X_VMEM_LIMIT_BYTES)


def _split_hi_lo(a):
    hi = a.astype(BF16)
    lo = (a - hi.astype(F32)).astype(BF16)
    return hi, lo


def _dot(a, b, hp, dims=(((1,), (0,)), ((), ()))):
    if not hp:
        return lax.dot_general(a.astype(BF16), b.astype(BF16), dims, preferred_element_type=F32)
    ah, al = _split_hi_lo(a.astype(F32))
    bh, bl = _split_hi_lo(b.astype(F32))
    d = functools.partial(lax.dot_general, dimension_numbers=dims, preferred_element_type=F32)
    return d(ah, bh) + (d(ah, bl) + d(al, bh))


_NT = (((1,), (1,)), ((), ()))
_TN = (((0,), (0,)), ((), ()))


def _mm_kernel(x_ref, w_ref, *out_refs, hp, scale):
    acc = _dot(x_ref[...], w_ref[...], hp)
    if scale is not None:
        acc = acc * scale
    for o in out_refs:
        o[...] = acc.astype(o.dtype)


def _mm(x, w, out_dtypes, *, hp=False, scale=None, tm=512, tn=1024):
    m, k = x.shape
    n = w.shape[1]
    tm = min(tm, m)
    tn = min(tn, n)
    assert m % tm == 0 and n % tn == 0
    outs = pl.pallas_call(
        functools.partial(_mm_kernel, hp=hp, scale=scale),
        grid=(n // tn, m // tm),
        in_specs=[pl.BlockSpec((tm, k), lambda j, i: (i, 0)),
                  pl.BlockSpec((k, tn), lambda j, i: (0, j))],
        out_specs=[pl.BlockSpec((tm, tn), lambda j, i: (i, j)) for _ in out_dtypes],
        out_shape=[jax.ShapeDtypeStruct((m, n), dt) for dt in out_dtypes],
        compiler_params=_cparams(("parallel", "parallel")),
        name="proj_hp" if hp else "proj",
    )(x, w)
    return outs


def _neg_softplus(z):
    return -(jnp.maximum(z, 0.0) + jnp.log1p(jnp.exp(-jnp.abs(z))))


def _strict_lower_ones(n):
    r = lax.broadcasted_iota(jnp.int32, (n, n), 0)
    c = lax.broadcasted_iota(jnp.int32, (n, n), 1)
    return jnp.where(r > c, 1.0, 0.0).astype(BF16)


def _sb_block(z, v, tri, carry, valid):
    lk = _neg_softplus(z)
    if valid is not None:
        lk = jnp.where(valid, lk, 0.0)
    hi, lo = _split_hi_lo(lk)
    tail = (jnp.dot(hi, tri, preferred_element_type=F32)
            + jnp.dot(lo, tri, preferred_element_type=F32)) + carry
    w = jnp.exp(z + lk + tail)
    if valid is not None:
        w = jnp.where(valid, w, 0.0)
    out = jnp.dot(w.astype(BF16), v, preferred_element_type=F32)
    return out, carry + jnp.sum(lk, axis=1, keepdims=True)


def _sb_prompt_kernel(bias_ref, q_ref, k_ref, v_ref, o_ref, acc_ref, *, tq, tk):
    h = pl.program_id(1)
    qi = pl.program_id(2)
    bias = bias_ref[h]
    q = q_ref[...]
    tri = _strict_lower_ones(tk)
    r = tq // tk
    acc_ref[...] = jnp.zeros_like(acc_ref)

    def step(j, carry, masked):
        start = pl.multiple_of(j * tk, tk)
        k = k_ref[pl.ds(start, tk), :]
        v = v_ref[pl.ds(start, tk), :]
        z = lax.dot_general(q, k, _NT, preferred_element_type=F32) + bias
        valid = None
        if masked:
            qpos = qi * tq + lax.broadcasted_iota(jnp.int32, (tq, tk), 0)
            kpos = j * tk + lax.broadcasted_iota(jnp.int32, (tq, tk), 1)
            valid = kpos < qpos
        out, carry = _sb_block(z, v, tri, carry, valid)
        acc_ref[...] += out
        return carry

    carry = jnp.zeros((tq, 1), F32)
    for d in range(r):
        carry = step(qi * r + (r - 1 - d), carry, True)
    n_full = qi * r
    lax.fori_loop(0, n_full, lambda t, c: step(n_full - 1 - t, c, False), carry)
    o_ref[...] = acc_ref[...].astype(o_ref.dtype)


def _sb_prompt(q, k, v, sb_bias, batch, seq, *, tq=256, tk=256):
    tq = min(tq, seq)
    tk = min(tk, tq)
    nq = seq // tq
    d = SB_HEAD_DIM
    return pl.pallas_call(
        functools.partial(_sb_prompt_kernel, tq=tq, tk=tk),
        grid=(batch, SB_HEADS, nq),
        in_specs=[pl.BlockSpec(memory_space=pltpu.SMEM),
                  pl.BlockSpec((tq, d), lambda b, h, i: (b * nq + i, h)),
                  pl.BlockSpec((seq, d), lambda b, h, i: (b, h)),
                  pl.BlockSpec((seq, d), lambda b, h, i: (b, h))],
        out_specs=pl.BlockSpec((tq, d), lambda b, h, i: (b * nq + i, h)),
        out_shape=jax.ShapeDtypeStruct(q.shape, BF16),
        scratch_shapes=[pltpu.VMEM((tq, d), F32)],
        compiler_params=_cparams(("parallel", "parallel", "arbitrary")),
        name="sb_prompt",
    )(sb_bias.astype(F32), q, k, v)


def _sb_sample_kernel(pt_ref, qbd_ref, brow_ref, knew_ref, vnew_ref, kc_ref, vc_ref, o_ref,
                      acc_ref, carry_ref, *, n_pages, dec_seq):
    j = pl.program_id(1)
    page = kc_ref.shape[0]
    tri = _strict_lower_ones(page)
    qbd = qbd_ref[...]

    def step(k, v, valid):
        z = lax.dot_general(qbd, k, _NT, preferred_element_type=F32) + brow_ref[...]
        out, carry = _sb_block(z, v, tri, carry_ref[...], valid)
        acc_ref[...] += out
        carry_ref[...] = carry

    @pl.when(j == 0)
    def _():
        acc_ref[...] = jnp.zeros_like(acc_ref)
        carry_ref[...] = jnp.zeros_like(carry_ref)
        rows = qbd.shape[0]
        t = lax.broadcasted_iota(jnp.int32, (rows, page), 0) % dec_seq
        s = lax.broadcasted_iota(jnp.int32, (rows, page), 1)
        step(knew_ref[...], vnew_ref[...], s < t)

    @pl.when(j > 0)
    def _():
        step(kc_ref[...].astype(BF16), vc_ref[...].astype(BF16), None)

    @pl.when(j == n_pages)
    def _():
        d = SB_HEAD_DIM
        for h in range(SB_HEADS):
            o_ref[:, h * d:(h + 1) * d] = acc_ref[h * dec_seq:(h + 1) * dec_seq, h * d:(h + 1) * d]


def _sb_sample(q_s, k_new, v_new, cache_k, cache_v, page_table, sb_bias):
    db, tn, w = q_s.shape
    n_pages = page_table.shape[1]
    page = cache_k.shape[1]
    d = SB_HEAD_DIM
    rows = SB_HEADS * tn
    qh = q_s.reshape(db, tn, SB_HEADS, d).transpose(0, 2, 1, 3)
    eye = jnp.eye(SB_HEADS, dtype=F32)
    qbd = (qh[:, :, :, None, :] * eye[None, :, None, :, None]).reshape(db, rows, w).astype(BF16)
    brow = jnp.repeat(sb_bias.astype(F32), tn).reshape(rows, 1)
    pad = ((0, 0), (0, page - tn), (0, 0))
    knew = jnp.pad(k_new, pad).astype(BF16)
    vnew = jnp.pad(v_new, pad).astype(BF16)

    def cache_map(b, j, pt):
        return (pt[b * n_pages + n_pages - jnp.maximum(j, 1)], 0, 0)

    grid_spec = pltpu.PrefetchScalarGridSpec(
        num_scalar_prefetch=1,
        grid=(db, n_pages + 1),
        in_specs=[pl.BlockSpec((None, rows, w), lambda b, j, pt: (b, 0, 0)),
                  pl.BlockSpec((rows, 1), lambda b, j, pt: (0, 0)),
                  pl.BlockSpec((None, page, w), lambda b, j, pt: (b, 0, 0)),
                  pl.BlockSpec((None, page, w), lambda b, j, pt: (b, 0, 0)),
                  pl.BlockSpec((None, page, w), cache_map),
                  pl.BlockSpec((None, page, w), cache_map)],
        out_specs=pl.BlockSpec((None, tn, w), lambda b, j, pt: (b, 0, 0)),
        scratch_shapes=[pltpu.VMEM((rows, w), F32), pltpu.VMEM((rows, 1), F32)],
    )
    return pl.pallas_call(
        functools.partial(_sb_sample_kernel, n_pages=n_pages, dec_seq=tn),
        grid_spec=grid_spec,
        out_shape=jax.ShapeDtypeStruct((db, tn, w), F32),
        compiler_params=_cparams(("parallel", "arbitrary")),
        name="sb_sample",
    )(page_table.reshape(-1).astype(jnp.int32), qbd, brow, knew, vnew, cache_k, cache_v)


def _ret_tables(positions, chunk_len, tile):
    half = RET_DK // 2
    inv = np.float32(ROPE_BASE) ** (-(np.arange(half, dtype=np.float32) / np.float32(half)))
    ang = positions.astype(np.float32)[:, None] * inv.astype(np.float32)[None, :]
    c = np.cos(ang.astype(np.float64)).astype(np.float32)
    s = np.sin(ang.astype(np.float64)).astype(np.float32)
    cos2 = np.concatenate([c, c], axis=1)
    sin2 = np.concatenate([-s, s], axis=1)
    hh = np.arange(RET_HEADS, dtype=np.float32)
    log_g = np.log(np.float32(1.0) - np.float32(2.0) ** (np.float32(-5.0) - hh)).astype(np.float32)
    i = np.arange(tile, dtype=np.float32)
    live = (np.arange(tile) < chunk_len)
    diff = i[:, None] - i[None, :]
    decay = np.where((diff >= 0)[None] & live[None, :, None] & live[None, None, :],
                     np.exp(np.maximum(diff, 0.0)[None] * log_g[:, None, None]), 0.0).astype(np.float32)
    q_dec = np.where(live[None], np.exp((i + 1.0)[None] * log_g[:, None]), 0.0).astype(np.float32)
    k_dec = np.where(live[None], np.exp((np.float32(chunk_len) - 1.0 - i)[None] * log_g[:, None]),
                     0.0).astype(np.float32)
    c_dec = np.exp(np.float32(chunk_len) * log_g).astype(np.float32)
    k_scale = np.float32(RET_DK) ** np.float32(-0.5)
    qdec_full = np.repeat(q_dec.T, RET_DK, axis=1)
    kdec_full = np.repeat(k_dec.T, RET_DK, axis=1)
    cdec_full = np.broadcast_to(c_dec[:, None, None], (RET_HEADS, 1, RET_DV)).copy()
    return (jnp.asarray(cos2), jnp.asarray(sin2), jnp.asarray(decay), jnp.asarray(qdec_full),
            jnp.asarray(kdec_full), jnp.asarray(cdec_full), float(k_scale))


def _retention_kernel(rq_ref, rk_ref, rv_ref, rg_ref, cos_ref, sin_ref, decay_ref, qdec_ref, kdec_ref,
                      cdec_ref, gn_ref, s0_ref, ret_ref, sfin_ref, state_ref, *, hp, k_scale):
    c = pl.program_id(1)

    @pl.when(c == 0)
    def _():
        state_ref[...] = s0_ref[...]

    cos = cos_ref[...]
    sin = sin_ref[...]

    def rope(x):
        return x * cos + pltpu.roll(x, RET_DK // 2, 1) * sin

    for h in range(RET_HEADS):
        ks = slice(h * RET_DK, (h + 1) * RET_DK)
        vs = slice(h * RET_DV, (h + 1) * RET_DV)
        q = rope(rq_ref[:, ks].astype(F32))
        k = rope(rk_ref[:, ks].astype(F32)) * k_scale
        v = rv_ref[:, vs]
        scores = _dot(q, k, hp, _NT) * decay_ref[h]
        intra = _dot(scores, v, hp)
        state = state_ref[h]
        inter = _dot(q * qdec_ref[:, ks], state, hp)
        state_ref[h] = state * cdec_ref[h] + _dot(k * kdec_ref[:, ks], v, hp, _TN)
        o = intra + inter
        mu = jnp.mean(o, axis=-1, keepdims=True)
        cen = o - mu
        var = jnp.mean(cen * cen, axis=-1, keepdims=True)
        g = rg_ref[:, vs].astype(F32)
        ret = cen * lax.rsqrt(var + GN_EPS) * gn_ref[:, vs] * (g * jax.nn.sigmoid(g))
        ret_ref[:, vs] = ret.astype(ret_ref.dtype)

    @pl.when(c == pl.num_programs(1) - 1)
    def _():
        sfin_ref[...] = state_ref[...]


def _retention(rq, rk, rv, rg, gn_g, s0, batch, n_chunks, tables, *, hp, out_dtype):
    cos2, sin2, decay, qdec, kdec, cdec, k_scale = tables
    tile = decay.shape[1]
    wk = RET_HEADS * RET_DK
    wv = RET_HEADS * RET_DV
    (rq, cq), (rk, ck), (rv, cv), (rg, cg) = rq, rk, rv, rg
    row = lambda b, c: (b * n_chunks + c, 0)
    rowc = lambda cb: (lambda b, c: (b * n_chunks + c, cb))
    const2 = lambda b, c: (0, 0)
    const3 = lambda b, c: (0, 0, 0)
    return pl.pallas_call(
        functools.partial(_retention_kernel, hp=hp, k_scale=k_scale),
        grid=(batch, n_chunks),
        in_specs=[pl.BlockSpec((tile, wk), rowc(cq)), pl.BlockSpec((tile, wk), rowc(ck)),
                  pl.BlockSpec((tile, wv), rowc(cv)), pl.BlockSpec((tile, wv), rowc(cg)),
                  pl.BlockSpec((tile, RET_DK), lambda b, c: (c, 0)),
                  pl.BlockSpec((tile, RET_DK), lambda b, c: (c, 0)),
                  pl.BlockSpec(decay.shape, const3),
                  pl.BlockSpec(qdec.shape, const2), pl.BlockSpec(kdec.shape, const2),
                  pl.BlockSpec(cdec.shape, const3),
                  pl.BlockSpec((1, wv), const2),
                  pl.BlockSpec((None, RET_HEADS, RET_DK, RET_DV), lambda b, c: (b, 0, 0, 0))],
        out_specs=[pl.BlockSpec((tile, wv), row),
                   pl.BlockSpec((None, RET_HEADS, RET_DK, RET_DV), lambda b, c: (b, 0, 0, 0))],
        out_shape=[jax.ShapeDtypeStruct((batch * n_chunks * tile, wv), out_dtype),
                   jax.ShapeDtypeStruct((batch, RET_HEADS, RET_DK, RET_DV), F32)],
        scratch_shapes=[pltpu.VMEM((RET_HEADS, RET_DK, RET_DV), F32)],
        compiler_params=_cparams(("parallel", "arbitrary")),
        name="retention_hp" if hp else "retention",
    )(rq, rk, rv, rg, cos2, sin2, decay, qdec, kdec, cdec, gn_g.reshape(1, wv).astype(F32), s0)


def _merge_kernel(osb_ref, ret_ref, wa_ref, wb_ref, ga_ref, gb_ref, m_ref, *, hp):
    pa = _dot(osb_ref[...], wa_ref[...], hp)
    pb = _dot(ret_ref[...], wb_ref[...], hp)
    m = jax.nn.sigmoid(ga_ref[...].astype(F32)) * pa + jax.nn.sigmoid(gb_ref[...].astype(F32)) * pb
    m_ref[...] = m.astype(m_ref.dtype)


def _merge(osb, ret, wa, wb, gab, *, hp, out_dtype, tm=512, tn=512):
    m, k = osb.shape
    n = wa.shape[1]
    tm = min(tm, m)
    nb = n // tn
    row = pl.BlockSpec((tm, k), lambda j, i: (i, 0))
    wsp = pl.BlockSpec((k, tn), lambda j, i: (0, j))
    tile = pl.BlockSpec((tm, tn), lambda j, i: (i, j))
    tile_b = pl.BlockSpec((tm, tn), lambda j, i: (i, nb + j))
    gb = gab
    ga = gab
    return pl.pallas_call(
        functools.partial(_merge_kernel, hp=hp),
        grid=(nb, m // tm),
        in_specs=[row, row, wsp, wsp, tile, tile_b],
        out_specs=tile,
        out_shape=jax.ShapeDtypeStruct((m, n), out_dtype),
        compiler_params=_cparams(("parallel", "parallel")),
        name="merge_hp" if hp else "merge",
    )(osb, ret, wa, wb, ga, gb)


def _layer_norm(x, g, b):
    mu = jnp.mean(x, axis=-1, keepdims=True)
    cen = x - mu
    var = jnp.mean(cen * cen, axis=-1, keepdims=True)
    return cen * lax.rsqrt(var + LN_EPS) * g + b


def _route(logits):
    lane = lax.broadcasted_iota(jnp.int32, logits.shape, 1)
    big = jnp.int32(ROUTE_LANES)
    neg = -jnp.inf

    def first_max(x):
        m = jnp.max(x, axis=1, keepdims=True)
        idx = jnp.min(jnp.where(x == m, lane, big), axis=1, keepdims=True)
        return m, idx

    gl = jnp.where(lane < N_GROUPS, logits, neg)
    gmax, gidx = first_max(gl)
    g_weight = 1.0 / jnp.sum(jnp.exp(gl - gmax), axis=1, keepdims=True)
    lo = N_GROUPS + gidx * EXPERTS_PER_GROUP
    el = jnp.where((lane >= lo) & (lane < lo + EXPERTS_PER_GROUP), logits, neg)
    v0, i0 = first_max(el)
    v1, i1 = first_max(jnp.where(lane == i0, neg, el))
    e = jnp.exp(v1 - v0)
    w0 = 1.0 / (1.0 + e)
    w1 = e / (1.0 + e)
    vals = [(i0 - N_GROUPS).astype(F32), (i1 - N_GROUPS).astype(F32), g_weight * w0, g_weight * w1]
    out = jnp.zeros(logits.shape, F32)
    for n, val in enumerate(vals):
        out = jnp.where(lane == n, val, out)
    return out


def _out_ln_kernel(m_ref, wo_ref, x_ref, g_ref, b_ref, wr_ref, br_ref, h_ref, route_ref, *, hp, alpha, n_real):
    i = pl.program_id(0)

    @pl.when(i < n_real)
    def _():
        mix = _dot(m_ref[...], wo_ref[...], hp)
        h = _layer_norm(alpha * x_ref[...] + mix, g_ref[...], b_ref[...])
        h_ref[...] = h
        logits = _dot(h, wr_ref[...], True) + br_ref[...]
        route_ref[...] = _route(logits)

    @pl.when(i >= n_real)
    def _():
        h_ref[...] = jnp.zeros_like(h_ref)


def _out_ln(m, w_out, x, ln_g, ln_b, w_route, b_route, *, hp, alpha, spare_tiles=0, tm=256):
    t, d = x.shape
    tm = min(tm, t)
    n_real = t // tm
    const = lambda i: (0, 0)
    row = lambda i: (jnp.minimum(i, n_real - 1), 0)
    return pl.pallas_call(
        functools.partial(_out_ln_kernel, hp=hp, alpha=alpha, n_real=n_real),
        grid=(n_real + spare_tiles,),
        in_specs=[pl.BlockSpec((tm, d), row), pl.BlockSpec((d, d), const), pl.BlockSpec((tm, d), row),
                  pl.BlockSpec((1, d), const), pl.BlockSpec((1, d), const),
                  pl.BlockSpec((d, ROUTE_LANES), const), pl.BlockSpec((1, ROUTE_LANES), const)],
        out_specs=[pl.BlockSpec((tm, d), lambda i: (i, 0)), pl.BlockSpec((tm, ROUTE_LANES), row)],
        out_shape=[jax.ShapeDtypeStruct((t + spare_tiles * tm, d), F32),
                   jax.ShapeDtypeStruct((t, ROUTE_LANES), F32)],
        compiler_params=_cparams(("arbitrary",)),
        name="out_ln_hp" if hp else "out_ln",
    )(m, w_out, x, ln_g.reshape(1, d), ln_b.reshape(1, d), w_route, b_route)


def _row_copy(src_hbm, row, dst, dst_row, sem):
    return pltpu.make_async_copy(src_hbm.at[pl.ds(row, 1), :], dst.at[pl.ds(dst_row, 1), :], sem)


def _gather_rows(src_hbm, idx_ref, n, dst, sem, wait):
    def body(r, carry):
        cp = _row_copy(src_hbm, idx_ref[0, r], dst, r, sem)
        if wait:
            cp.wait()
        else:
            cp.start()
        return carry
    lax.fori_loop(0, n, body, 0, unroll=8)


def _moe_ffn_kernel(te_ref, nt_ref, idx_ref, idx_next_ref, gate_ref, h_hbm, w1_ref, w3_ref, w2_ref,
                    y_ref, xbuf, sem):
    t = pl.program_id(0)
    n_tiles = nt_ref[0]
    slot = t % 2
    rows = xbuf.shape[1]

    @pl.when(t == 0)
    def _():
        _gather_rows(h_hbm, idx_ref, rows, xbuf.at[0], sem.at[0], wait=False)

    @pl.when(t + 1 < n_tiles)
    def _():
        _gather_rows(h_hbm, idx_next_ref, rows, xbuf.at[1 - slot], sem.at[1 - slot], wait=False)

    @pl.when(t < n_tiles)
    def _():
        _gather_rows(h_hbm, idx_ref, rows, xbuf.at[slot], sem.at[slot], wait=True)
        x = xbuf[slot].astype(BF16)
        a = jnp.dot(x, w1_ref[...].astype(BF16), preferred_element_type=F32)
        c = jnp.dot(x, w3_ref[...].astype(BF16), preferred_element_type=F32)
        hid = (a * jax.nn.sigmoid(a)) * c * gate_ref[...]
        y_ref[...] = jnp.dot(hid.astype(BF16), w2_ref[...].astype(BF16), preferred_element_type=F32)

    @pl.when(t >= n_tiles)
    def _():
        y_ref[...] = jnp.zeros_like(y_ref)


def _moe_ffn(h_all, tile_expert, n_tiles, slot_token, slot_gate, w1, w3, w2):
    nt = slot_token.shape[0]
    d = h_all.shape[1]
    ff = w1.shape[2]
    last = lambda t, nv: jnp.minimum(t, nv[0] - 1)
    grid_spec = pltpu.PrefetchScalarGridSpec(
        num_scalar_prefetch=2,
        grid=(nt,),
        in_specs=[
            pl.BlockSpec((None, 1, MOE_TILE), lambda t, te, nv: (last(t, nv), 0, 0), memory_space=pltpu.SMEM),
            pl.BlockSpec((None, 1, MOE_TILE), lambda t, te, nv: (last(t + 1, nv), 0, 0),
                         memory_space=pltpu.SMEM),
            pl.BlockSpec((MOE_TILE, 1), lambda t, te, nv: (last(t, nv), 0)),
            pl.BlockSpec(memory_space=pl.ANY),
            pl.BlockSpec((None, d, ff), lambda t, te, nv: (te[t], 0, 0)),
            pl.BlockSpec((None, d, ff), lambda t, te, nv: (te[t], 0, 0)),
            pl.BlockSpec((None, ff, d), lambda t, te, nv: (te[t], 0, 0)),
        ],
        out_specs=pl.BlockSpec((MOE_TILE, d), lambda t, te, nv: (t, 0)),
        scratch_shapes=[pltpu.VMEM((2, MOE_TILE, d), F32), pltpu.SemaphoreType.DMA((2,))],
    )
    return pl.pallas_call(
        _moe_ffn_kernel,
        grid_spec=grid_spec,
        out_shape=jax.ShapeDtypeStruct((nt * MOE_TILE, d), F32),
        compiler_params=_cparams(("arbitrary",)),
        name="moe_ffn",
    )(tile_expert, n_tiles, slot_token, slot_token, slot_gate, h_all, w1, w3, w2)


def _combine_ln_kernel(idx_ref, idx_next_ref, h_ref, g_ref, b_ref, y_hbm, o_ref, ybuf, sem, *, alpha):
    t = pl.program_id(0)
    nt = pl.num_programs(0)
    slot = t % 2
    rows = ybuf.shape[1]

    @pl.when(t == 0)
    def _():
        _gather_rows(y_hbm, idx_ref, rows, ybuf.at[0], sem.at[0], wait=False)

    @pl.when(t + 1 < nt)
    def _():
        _gather_rows(y_hbm, idx_next_ref, rows, ybuf.at[1 - slot], sem.at[1 - slot], wait=False)

    _gather_rows(y_hbm, idx_ref, rows, ybuf.at[slot], sem.at[slot], wait=True)
    tm = rows // 2
    moe = ybuf[slot, pl.ds(0, tm), :] + ybuf[slot, pl.ds(tm, tm), :]
    o_ref[...] = _layer_norm(alpha * h_ref[...] + moe, g_ref[...], b_ref[...])


def _combine_ln(h_all, row0, n_rows, tok_slots, y_sorted, ln_g, ln_b, *, alpha, tm):
    d = h_all.shape[1]
    nt = n_rows // tm
    b0 = row0 // tm
    const = lambda t: (0, 0)
    return pl.pallas_call(
        functools.partial(_combine_ln_kernel, alpha=alpha),
        grid=(nt,),
        in_specs=[pl.BlockSpec((None, 1, 2 * tm), lambda t: (t, 0, 0), memory_space=pltpu.SMEM),
                  pl.BlockSpec((None, 1, 2 * tm), lambda t: (jnp.minimum(t + 1, nt - 1), 0, 0),
                               memory_space=pltpu.SMEM),
                  pl.BlockSpec((tm, d), lambda t: (b0 + t, 0)),
                  pl.BlockSpec((1, d), const), pl.BlockSpec((1, d), const),
                  pl.BlockSpec(memory_space=pl.ANY)],
        out_specs=pl.BlockSpec((tm, d), lambda t: (t, 0)),
        out_shape=jax.ShapeDtypeStruct((n_rows, d), F32),
        scratch_shapes=[pltpu.VMEM((2, 2 * tm, d), F32), pltpu.SemaphoreType.DMA((2,))],
        compiler_params=_cparams(("arbitrary",)),
        name="combine_ln",
    )(tok_slots, tok_slots, h_all, ln_g.reshape(1, d), ln_b.reshape(1, d), y_sorted)


def _moe_plan(route, n_tokens):
    e = jnp.concatenate([route[:, 0], route[:, 1]]).astype(jnp.int32)
    gate = jnp.concatenate([route[:, 2], route[:, 3]])
    tok = jnp.concatenate([jnp.arange(n_tokens, dtype=jnp.int32)] * 2)
    n_assign = 2 * n_tokens
    nt = (n_assign + N_EXPERTS * (MOE_TILE - 1)) // MOE_TILE
    onehot = (e[:, None] == jnp.arange(N_EXPERTS, dtype=jnp.int32)[None, :]).astype(jnp.int32)
    csum = jnp.cumsum(onehot, axis=0)
    counts = csum[-1]
    rank = jnp.take_along_axis(csum, e[:, None], axis=1)[:, 0] - 1
    tiles_per_e = (counts + MOE_TILE - 1) // MOE_TILE
    tile_end = jnp.cumsum(tiles_per_e)
    tile_start = tile_end - tiles_per_e
    n_tiles = tile_end[-1:]
    slot = tile_start[e] * MOE_TILE + rank
    n_slots = nt * MOE_TILE
    slot_token = jnp.zeros((n_slots,), jnp.int32).at[slot].set(tok)
    slot_gate = jnp.zeros((n_slots,), F32).at[slot].set(gate)
    tile_ids = jnp.minimum(jnp.arange(nt, dtype=jnp.int32), n_tiles[0] - 1)
    tile_expert = jnp.minimum(jnp.searchsorted(tile_end, tile_ids, side="right"), N_EXPERTS - 1).astype(jnp.int32)
    return (tile_expert, n_tiles.astype(jnp.int32), slot_token.reshape(nt, 1, MOE_TILE),
            slot_gate.reshape(n_slots, 1), slot[:n_tokens], slot[n_tokens:])


def _tile_slots(slot_a, slot_b, tm):
    n = slot_a.shape[0] // tm
    return jnp.concatenate([slot_a.reshape(n, 1, tm), slot_b.reshape(n, 1, tm)], axis=2)


def kernel(x_prompt, x_sample, cache_k, cache_v, state_ret, page_table, w_in, sb_bias, w_branch_a, w_branch_b,
           w_out, gn_g, ln1_g, ln1_b, ln2_g, ln2_b, w_group_router, b_group_router, w_expert_router,
           b_expert_router, w1, w3, w2):
    depth = w_in.shape[0]
    assert depth == 1, "single-layer step"
    batch, seq, d = x_prompt.shape
    db, tn, _ = x_sample.shape
    sb_w = SB_HEADS * SB_HEAD_DIM
    qk_w = RET_HEADS * RET_DK
    v_w = RET_HEADS * RET_DV
    widths = (sb_w, sb_w, sb_w, qk_w, qk_w, v_w, v_w, d, d)
    offs = np.concatenate([[0], np.cumsum(widths)])
    o_q, o_k, o_rq, o_rv, o_ga, o_end = offs[0], offs[1], offs[3], offs[5], offs[7], offs[9]
    alpha = float((2.0 * depth) ** 0.25)
    sb_scale = float(SB_HEAD_DIM ** -0.5)
    n_pages = page_table.shape[1]
    page = cache_k.shape[2]
    past = n_pages * page

    w_in0 = w_in[0]
    wa, wb, wo = w_branch_a[0], w_branch_b[0], w_out[0]
    bias = sb_bias[0]
    gn = gn_g[0]
    w_route = jnp.concatenate(
        [w_group_router[0], w_expert_router[0].reshape(d, N_EXPERTS),
         jnp.zeros((d, ROUTE_LANES - N_GROUPS - N_EXPERTS), F32)], axis=1)
    b_route = jnp.concatenate(
        [b_group_router[0], b_expert_router[0].reshape(N_EXPERTS),
         jnp.zeros((ROUTE_LANES - N_GROUPS - N_EXPERTS,), F32)]).reshape(1, ROUTE_LANES)

    t_p = batch * seq
    t_s = db * tn
    t_all = t_p + t_s

    xp = x_prompt.reshape(t_p, d)
    xp16 = xp.astype(BF16)
    wq16 = w_in0[:, o_q:o_k].astype(BF16)
    wk16 = w_in0[:, o_k:o_k + sb_w].astype(BF16)
    wv16 = w_in0[:, o_k + sb_w:o_rq].astype(BF16)
    wr16 = w_in0[:, o_rq:o_rv].astype(BF16)
    wvg16 = w_in0[:, o_rv:o_ga].astype(BF16)
    wgg16 = w_in0[:, o_ga:o_end].astype(BF16)
    (q16,) = _mm(xp16, wq16, (BF16,), scale=sb_scale)
    k32, k16 = _mm(xp16, wk16, (F32, BF16))
    v32, v16 = _mm(xp16, wv16, (F32, BF16))
    (rqk,) = _mm(xp16, wr16, (F32,))
    (rvg,) = _mm(xp16, wvg16, (BF16,))
    (gab,) = _mm(xp16, wgg16, (BF16,))

    o_sb = _sb_prompt(q16, k16, v16, bias, batch, seq)

    n_chunks = seq // RET_CHUNK
    tables_p = _ret_tables(np.arange(seq), RET_CHUNK, RET_CHUNK)
    s0_p = jnp.zeros((batch, RET_HEADS, RET_DK, RET_DV), F32)
    ret_p, st_p = _retention((rqk, 0), (rqk, 1), (rvg, 0), (rvg, 1), gn, s0_p, batch, n_chunks, tables_p,
                             hp=False, out_dtype=BF16)
    m_p = _merge(o_sb, ret_p, wa.astype(BF16), wb.astype(BF16), gab, hp=False, out_dtype=BF16)
    tm_h = 256
    assert t_s <= tm_h
    h_all, route_p = _out_ln(m_p, wo.astype(BF16), xp, ln1_g[0], ln1_b[0], w_route, b_route,
                             hp=False, alpha=alpha, spare_tiles=1, tm=tm_h)

    xs = x_sample.reshape(t_s, d)
    (p_s,) = _mm(xs, w_in0, (F32,), hp=True, tn=512)
    k_s = p_s[:, o_k:o_k + sb_w]
    v_s = p_s[:, o_k + sb_w:o_rq]
    q_s = p_s[:, o_q:o_k] * sb_scale
    o_sb_s = _sb_sample(q_s.reshape(db, tn, sb_w), k_s.reshape(db, tn, sb_w), v_s.reshape(db, tn, sb_w),
                        cache_k.reshape(-1, page, sb_w), cache_v.reshape(-1, page, sb_w), page_table, bias)

    def pad_rows(a):
        return jnp.pad(a.reshape(db, tn, -1), ((0, 0), (0, RET_CHUNK - tn), (0, 0))).reshape(db * RET_CHUNK, -1)

    pos_s = past + np.arange(RET_CHUNK)
    tables_s = _ret_tables(pos_s, tn, RET_CHUNK)
    rqk_s = pad_rows(p_s[:, o_rq:o_rv])
    rvg_s = pad_rows(p_s[:, o_rv:o_ga])
    ret_s_pad, st_s = _retention((rqk_s, 0), (rqk_s, 1), (rvg_s, 0), (rvg_s, 1), gn, state_ret[0], db, 1,
                                 tables_s, hp=True, out_dtype=F32)
    ret_s = ret_s_pad.reshape(db, RET_CHUNK, v_w)[:, :tn].reshape(t_s, v_w)
    m_s = _merge(o_sb_s.reshape(t_s, sb_w), ret_s, wa, wb, p_s[:, o_ga:o_end], hp=True, out_dtype=F32)
    h_s, route_s = _out_ln(m_s, wo, xs, ln1_g[0], ln1_b[0], w_route, b_route, hp=True, alpha=alpha)

    h_all = lax.dynamic_update_slice(h_all, h_s, (t_p, 0))
    route = jnp.concatenate([route_p, route_s], axis=0)
    tile_expert, n_tiles, slot_token, slot_gate, slot_a, slot_b = _moe_plan(route, t_all)
    y_sorted = _moe_ffn(h_all, tile_expert, n_tiles, slot_token, slot_gate, w1[0], w3[0], w2[0])
    tm_p = 256
    y_p = _combine_ln(h_all, 0, t_p, _tile_slots(slot_a[:t_p], slot_b[:t_p], tm_p), y_sorted,
                      ln2_g[0], ln2_b[0], alpha=alpha, tm=tm_p)
    y_s = _combine_ln(h_all, t_p, t_s, _tile_slots(slot_a[t_p:], slot_b[t_p:], t_s), y_sorted,
                      ln2_g[0], ln2_b[0], alpha=alpha, tm=t_s)

    hs = (SB_HEADS, SB_HEAD_DIM)
    return (y_p.reshape(batch, seq, d), y_s.reshape(db, tn, d),
            k32.reshape(1, batch, seq, *hs), v32.reshape(1, batch, seq, *hs),
            st_p[None],
            k_s.reshape(1, db, tn, *hs), v_s.reshape(1, db, tn, *hs),
            st_s[None])
```

```python
import functools

import numpy as np
import jax
import jax.numpy as jnp
from jax import lax
from jax.experimental import pallas as pl
from jax.experimental.pallas import tpu as pltpu

F32 = jnp.float32
BF16 = jnp.bfloat16

SB_HEADS = 16
SB_HEAD_DIM = 128
RET_HEADS = 8
RET_DK = 128
RET_DV = 256
RET_CHUNK = 128
ROPE_BASE = 10000.0
N_GROUPS = 4
EXPERTS_PER_GROUP = 8
N_EXPERTS = N_GROUPS * EXPERTS_PER_GROUP
LN_EPS = 1e-5
GN_EPS = 1e-5

V7X_VMEM_LIMIT_BYTES = 56 * 1024 * 1024
LANES = 128
MOE_TILE = 256
ROUTE_LANES = LANES


def _cparams(sem):
    return pltpu.CompilerParams(dimension_semantics=sem, vmem_limit_bytes=V7X_VMEM_LIMIT_BYTES)


def _split_hi_lo(a):
    hi = a.astype(BF16)
    lo = (a - hi.astype(F32)).astype(BF16)
    return hi, lo


def _dot(a, b, hp, dims=(((1,), (0,)), ((), ()))):
    if not hp:
        return lax.dot_general(a.astype(BF16), b.astype(BF16), dims, preferred_element_type=F32)
    ah, al = _split_hi_lo(a.astype(F32))
    bh, bl = _split_hi_lo(b.astype(F32))
    d = functools.partial(lax.dot_general, dimension_numbers=dims, preferred_element_type=F32)
    return d(ah, bh) + (d(ah, bl) + d(al, bh))


_NT = (((1,), (1,)), ((), ()))
_TN = (((0,), (0,)), ((), ()))


def _mm_kernel(x_ref, w_ref, *refs, hp, scale):
    if hp:
        out_refs = refs
        acc = _dot(x_ref[...], w_ref[...], True)
    else:
        *out_refs, w16_ref = refs

        @pl.when(pl.program_id(1) == 0)
        def _():
            w16_ref[...] = w_ref[...].astype(BF16)

        acc = jnp.dot(x_ref[...], w16_ref[...], preferred_element_type=F32)
    if scale is not None:
        acc = acc * scale
    for o in out_refs:
        o[...] = acc.astype(o.dtype)


def _mm(x, w, col0, n, out_dtypes, *, hp=False, scale=None, tm=512, tn=1024):
    m, k = x.shape
    col0, n = int(col0), int(n)
    tm = min(tm, m)
    tn = min(tn, n)
    assert m % tm == 0 and n % tn == 0 and col0 % tn == 0
    j0 = col0 // tn
    outs = pl.pallas_call(
        functools.partial(_mm_kernel, hp=hp, scale=scale),
        grid=(n // tn, m // tm),
        in_specs=[pl.BlockSpec((tm, k), lambda j, i: (i, 0)),
                  pl.BlockSpec((k, tn), lambda j, i: (0, j0 + j))],
        out_specs=[pl.BlockSpec((tm, tn), lambda j, i: (i, j)) for _ in out_dtypes],
        out_shape=[jax.ShapeDtypeStruct((m, n), dt) for dt in out_dtypes],
        scratch_shapes=[] if hp else [pltpu.VMEM((k, tn), BF16)],
        compiler_params=_cparams(("parallel", "arbitrary")),
        name="proj_hp" if hp else "proj",
    )(x, w)
    return outs


LOG2_E = 1.4426950408889634


EXP2_MAX_ARG = 126.0


def _softplus2(z2):
    return jnp.maximum(jnp.log2(1.0 + jnp.exp2(jnp.minimum(z2, EXP2_MAX_ARG))), z2)


def _strict_lower_ones(n):
    r = lax.broadcasted_iota(jnp.int32, (n, n), 0)
    c = lax.broadcasted_iota(jnp.int32, (n, n), 1)
    return jnp.where(r > c, 1.0, 0.0).astype(BF16)


SB_SUB = 256


def _sb_logit_stage(z2, valid, sub):
    s = _softplus2(z2)
    zs = z2 - s
    if valid is not None:
        s = jnp.where(valid, s, 0.0)
        zs = jnp.where(valid, zs, -jnp.inf)
    width = z2.shape[1]
    sub = min(sub, width)
    tri = _strict_lower_ones(sub)
    tails = []
    run = None
    for c in reversed(range(width // sub)):
        sc = s[:, c * sub:(c + 1) * sub]
        t = jnp.dot(sc.astype(BF16), tri, preferred_element_type=F32)
        rs = jnp.sum(sc, axis=1, keepdims=True)
        if run is None:
            run = rs
        else:
            t = t + run
            run = run + rs
        tails.append(t)
    return zs, jnp.concatenate(tails[::-1], axis=1), run


def _sb_weights(zs, tail):
    return jnp.exp2(zs - tail).astype(BF16)


def _sb_prompt_kernel(bias_ref, q_ref, k_ref, v_ref, o_ref, acc_ref, zs_ref, tail_ref, carry_ref,
                      rowsum_ref, *, tq):
    h = pl.program_id(1)
    qi = pl.program_id(2)
    lane = lax.broadcasted_iota(jnp.int32, (tq, SB_HEAD_DIM), 1)
    b = jnp.full((tq, SB_HEAD_DIM), bias_ref[h], F32)
    p0 = b.astype(BF16).astype(F32)
    p1 = (b - p0).astype(BF16).astype(F32)
    p2 = ((b - p0) - p1).astype(BF16).astype(F32)
    q_bias = jnp.where(lane == 0, p0, jnp.where(lane == 1, p1, jnp.where(lane == 2, p2, 0.0))).astype(BF16)
    k_ones = jnp.where(lane < 3, 1.0, 0.0).astype(BF16)
    q = jnp.concatenate([q_ref[...], q_bias], axis=1)
    acc_ref[...] = jnp.zeros_like(acc_ref)
    carry_ref[...] = jnp.zeros_like(carry_ref)

    def rows(j):
        return pl.ds(pl.multiple_of(j * tq, tq), tq)

    def logits(j, masked, p):
        k = jnp.concatenate([k_ref[rows(j), :], k_ones], axis=1)
        z2 = lax.dot_general(q, k, _NT, preferred_element_type=F32)
        valid = None
        if masked:
            valid = (lax.broadcasted_iota(jnp.int32, (tq, tq), 1)
                     < lax.broadcasted_iota(jnp.int32, (tq, tq), 0))
        zs_ref[p], tail_ref[p], rowsum_ref[p] = _sb_logit_stage(z2, valid, SB_SUB)

    def weights(j, p):
        w = _sb_weights(zs_ref[p], tail_ref[p])
        carry = carry_ref[...]
        acc_ref[...] += jnp.dot(w, v_ref[rows(j), :], preferred_element_type=F32) * jnp.exp2(-carry)
        carry_ref[...] = carry + rowsum_ref[p]

    def step(m, p):
        weights(qi - m, p)
        logits(qi - m - 1, False, 1 - p)

    logits(qi, True, 0)

    def pair(t, c):
        step(2 * t, 0)
        step(2 * t + 1, 1)
        return c

    lax.fori_loop(0, qi // 2, pair, 0)

    @pl.when(qi % 2 == 1)
    def _():
        step(qi - 1, 0)

    weights(0, qi % 2)
    o_ref[...] = acc_ref[...].astype(o_ref.dtype)


def _sb_prompt(q, k, v, sb_bias, batch, seq, *, tq=512):
    tq = min(tq, seq)
    nq = seq // tq
    d = SB_HEAD_DIM
    return pl.pallas_call(
        functools.partial(_sb_prompt_kernel, tq=tq),
        grid=(batch, SB_HEADS, nq),
        in_specs=[pl.BlockSpec(memory_space=pltpu.SMEM),
                  pl.BlockSpec((tq, d), lambda b, h, i: (b * nq + i, h)),
                  pl.BlockSpec((seq, d), lambda b, h, i: (b, h)),
                  pl.BlockSpec((seq, d), lambda b, h, i: (b, h))],
        out_specs=pl.BlockSpec((tq, d), lambda b, h, i: (b * nq + i, h)),
        out_shape=jax.ShapeDtypeStruct(q.shape, BF16),
        scratch_shapes=[pltpu.VMEM((tq, d), F32), pltpu.VMEM((2, tq, tq), F32), pltpu.VMEM((2, tq, tq), F32),
                        pltpu.VMEM((tq, 1), F32), pltpu.VMEM((2, tq, 1), F32)],
        compiler_params=_cparams(("parallel", "parallel", "arbitrary")),
        name="sb_prompt",
    )(sb_bias.astype(F32), q, k, v)


SB_SAMPLE_PAGES = 4


def _sb_sample_kernel(pt_ref, qbd_ref, brow_ref, expand_ref, hmask_ref, knew_ref, vnew_ref, *refs,
                      n_groups, dec_seq):
    npg = SB_SAMPLE_PAGES
    k_refs, v_refs = refs[:npg], refs[npg:2 * npg]
    o_ref, acc_ref, carry_ref = refs[2 * npg:]
    g = pl.program_id(1)
    qbd = qbd_ref[...]
    n_rows = qbd.shape[0]
    page = expand_ref.shape[0]

    def attend(k_all, v_rows, valid):
        z2 = lax.dot_general(qbd, k_all, _NT, preferred_element_type=F32) + brow_ref[...]
        zs, tail, rowsum = _sb_logit_stage(z2, valid, SB_SUB)
        w = _sb_weights(zs, tail)
        spread = [jnp.dot(w[:, p * page:(p + 1) * page], expand_ref[...], preferred_element_type=F32)
                  * hmask_ref[...] for p in range(w.shape[1] // page)]
        w_rows = jnp.concatenate(spread, axis=1).astype(BF16)
        carry = carry_ref[...]
        acc_ref[...] += jnp.dot(w_rows, v_rows, preferred_element_type=F32) * jnp.exp2(-carry)
        carry_ref[...] = carry + rowsum

    @pl.when(g == 0)
    def _():
        acc_ref[...] = jnp.zeros_like(acc_ref)
        carry_ref[...] = jnp.zeros_like(carry_ref)
        t = lax.broadcasted_iota(jnp.int32, (n_rows, page), 0) % dec_seq
        s = lax.broadcasted_iota(jnp.int32, (n_rows, page), 1)
        attend(knew_ref[...], vnew_ref[...], s < t)

    @pl.when(g > 0)
    def _():
        def k_page(ref):
            x = jnp.swapaxes(ref[...].astype(BF16), 0, 1)
            return jnp.concatenate([x[h] for h in range(SB_HEADS)], axis=1)

        k_all = jnp.concatenate([k_page(r) for r in k_refs], axis=0)
        v_rows = jnp.concatenate([r[...].astype(BF16) for r in v_refs], axis=0)
        attend(k_all, v_rows, None)

    @pl.when(g == n_groups)
    def _():
        o_ref[...] = acc_ref[...]


def _sb_sample(q_s, k_new, v_new, cache_k, cache_v, page_table, sb_bias):
    db, tn, w = q_s.shape
    n_pages = page_table.shape[1]
    n_pool, page = cache_k.shape[:2]
    nh, d = SB_HEADS, SB_HEAD_DIM
    npg = SB_SAMPLE_PAGES
    assert n_pages % npg == 0 and tn <= page
    n_groups = n_pages // npg
    n_rows = nh * tn
    qh = q_s.reshape(db, tn, nh, d).transpose(0, 2, 1, 3)
    eye = jnp.eye(nh, dtype=F32)
    qbd = (qh[:, :, :, None, :] * eye[None, :, None, :, None]).reshape(db, n_rows, w).astype(BF16)
    brow = jnp.repeat(sb_bias.astype(F32), tn).reshape(n_rows, 1)
    expand = jnp.asarray(np.repeat(np.eye(page, dtype=np.float32), nh, axis=1), BF16)
    hmask = jnp.asarray(np.tile(np.repeat(np.eye(nh, dtype=np.float32), tn, axis=0), (1, page)))
    pad = ((0, 0), (0, page - tn), (0, 0))
    knew = jnp.pad(k_new, pad).astype(BF16)
    vnew = jnp.pad(v_new, pad).astype(BF16).reshape(db, page * nh, d)
    v_rows = cache_v.reshape(n_pool, page * nh, d)

    def cache_map(p):
        return lambda b, g, pt: (pt[b * n_pages + n_pages - jnp.maximum(g, 1) * npg + p], 0, 0, 0)

    def cache_map_v(p):
        return lambda b, g, pt: (pt[b * n_pages + n_pages - jnp.maximum(g, 1) * npg + p], 0, 0)

    per_b = lambda b, g, pt: (b, 0, 0)
    const = lambda b, g, pt: (0, 0)
    grid_spec = pltpu.PrefetchScalarGridSpec(
        num_scalar_prefetch=1,
        grid=(db, n_groups + 1),
        in_specs=[pl.BlockSpec((None, n_rows, w), per_b),
                  pl.BlockSpec((n_rows, 1), const),
                  pl.BlockSpec((page, page * nh), const),
                  pl.BlockSpec((n_rows, page * nh), const),
                  pl.BlockSpec((None, page, w), per_b),
                  pl.BlockSpec((None, page * nh, d), per_b)]
        + [pl.BlockSpec((None, page, nh, d), cache_map(p)) for p in range(npg)]
        + [pl.BlockSpec((None, page * nh, d), cache_map_v(p)) for p in range(npg)],
        out_specs=pl.BlockSpec((None, n_rows, d), per_b),
        scratch_shapes=[pltpu.VMEM((n_rows, d), F32), pltpu.VMEM((n_rows, 1), F32)],
    )
    out = pl.pallas_call(
        functools.partial(_sb_sample_kernel, n_groups=n_groups, dec_seq=tn),
        grid_spec=grid_spec,
        out_shape=jax.ShapeDtypeStruct((db, n_rows, d), F32),
        compiler_params=_cparams(("parallel", "arbitrary")),
        name="sb_sample",
    )(page_table.reshape(-1).astype(jnp.int32), qbd, brow, expand, hmask, knew, vnew,
      *([cache_k] * npg), *([v_rows] * npg))
    return out.reshape(db, nh, tn, d).transpose(0, 2, 1, 3).reshape(db, tn, w)


def _ret_tables(positions, chunk_len, tile):
    half = RET_DK // 2
    inv = np.float32(ROPE_BASE) ** (-(np.arange(half, dtype=np.float32) / np.float32(half)))
    ang = positions.astype(np.float32)[:, None] * inv.astype(np.float32)[None, :]
    c = np.cos(ang.astype(np.float64)).astype(np.float32)
    s = np.sin(ang.astype(np.float64)).astype(np.float32)
    cos2 = np.concatenate([c, c], axis=1)
    sin2 = np.concatenate([-s, s], axis=1)
    hh = np.arange(RET_HEADS, dtype=np.float32)
    log_g = np.log(np.float32(1.0) - np.float32(2.0) ** (np.float32(-5.0) - hh)).astype(np.float32)
    i = np.arange(tile, dtype=np.float32)
    live = (np.arange(tile) < chunk_len)
    diff = i[:, None] - i[None, :]
    decay = np.where((diff >= 0)[None] & live[None, :, None] & live[None, None, :],
                     np.exp(np.maximum(diff, 0.0)[None] * log_g[:, None, None]), 0.0).astype(np.float32)
    q_dec = np.where(live[None], np.exp((i + 1.0)[None] * log_g[:, None]), 0.0).astype(np.float32)
    k_dec = np.where(live[None], np.exp((np.float32(chunk_len) - 1.0 - i)[None] * log_g[:, None]),
                     0.0).astype(np.float32)
    c_dec = np.exp(np.float32(chunk_len) * log_g).astype(np.float32)
    k_scale = np.float32(RET_DK) ** np.float32(-0.5)
    qdec_full = np.repeat(q_dec.T, RET_DK, axis=1)
    kdec_full = np.repeat(k_dec.T, RET_DK, axis=1)
    cdec_full = np.broadcast_to(c_dec[:, None, None], (RET_HEADS, 1, RET_DV)).copy()
    return (jnp.asarray(cos2), jnp.asarray(sin2), jnp.asarray(decay), jnp.asarray(qdec_full),
            jnp.asarray(kdec_full), jnp.asarray(cdec_full), float(k_scale))


def _retention_kernel(rq_ref, rk_ref, rv_ref, rg_ref, cos_ref, sin_ref, decay_ref, qdec_ref, kdec_ref,
                      cdec_ref, gn_ref, s0_ref, ret_ref, sfin_ref, state_ref, *, hp, k_scale):
    c = pl.program_id(1)

    @pl.when(c == 0)
    def _():
        state_ref[...] = s0_ref[...]

    cos = cos_ref[...]
    sin = sin_ref[...]

    def rope(x):
        return x * cos + pltpu.roll(x, RET_DK // 2, 1) * sin

    for h in range(RET_HEADS):
        ks = slice(h * RET_DK, (h + 1) * RET_DK)
        vs = slice(h * RET_DV, (h + 1) * RET_DV)
        q = rope(rq_ref[:, ks].astype(F32))
        k = rope(rk_ref[:, ks].astype(F32)) * k_scale
        v = rv_ref[:, vs]
        scores = _dot(q, k, hp, _NT) * decay_ref[h]
        intra = _dot(scores, v, hp)
        state = state_ref[h]
        inter = _dot(q * qdec_ref[:, ks], state, hp)
        state_ref[h] = state * cdec_ref[h] + _dot(k * kdec_ref[:, ks], v, hp, _TN)
        o = intra + inter
        mu = jnp.mean(o, axis=-1, keepdims=True)
        cen = o - mu
        var = jnp.mean(cen * cen, axis=-1, keepdims=True)
        g = rg_ref[:, vs].astype(F32)
        ret = cen * lax.rsqrt(var + GN_EPS) * gn_ref[:, vs] * (g * jax.nn.sigmoid(g))
        ret_ref[:, vs] = ret.astype(ret_ref.dtype)

    @pl.when(c == pl.num_programs(1) - 1)
    def _():
        sfin_ref[...] = state_ref[...]


def _retention(rq, rk, rv, rg, gn_g, s0, batch, n_chunks, tables, *, hp, out_dtype):
    cos2, sin2, decay, qdec, kdec, cdec, k_scale = tables
    tile = decay.shape[1]
    wk = RET_HEADS * RET_DK
    wv = RET_HEADS * RET_DV
    (rq, cq), (rk, ck), (rv, cv), (rg, cg) = rq, rk, rv, rg
    row = lambda b, c: (b * n_chunks + c, 0)
    rowc = lambda cb: (lambda b, c: (b * n_chunks + c, cb))
    const2 = lambda b, c: (0, 0)
    const3 = lambda b, c: (0, 0, 0)
    return pl.pallas_call(
        functools.partial(_retention_kernel, hp=hp, k_scale=k_scale),
        grid=(batch, n_chunks),
        in_specs=[pl.BlockSpec((tile, wk), rowc(cq)), pl.BlockSpec((tile, wk), rowc(ck)),
                  pl.BlockSpec((tile, wv), rowc(cv)), pl.BlockSpec((tile, wv), rowc(cg)),
                  pl.BlockSpec((tile, RET_DK), lambda b, c: (c, 0)),
                  pl.BlockSpec((tile, RET_DK), lambda b, c: (c, 0)),
                  pl.BlockSpec(decay.shape, const3),
                  pl.BlockSpec(qdec.shape, const2), pl.BlockSpec(kdec.shape, const2),
                  pl.BlockSpec(cdec.shape, const3),
                  pl.BlockSpec((1, wv), const2),
                  pl.BlockSpec((None, RET_HEADS, RET_DK, RET_DV), lambda b, c: (b, 0, 0, 0))],
        out_specs=[pl.BlockSpec((tile, wv), row),
                   pl.BlockSpec((None, RET_HEADS, RET_DK, RET_DV), lambda b, c: (b, 0, 0, 0))],
        out_shape=[jax.ShapeDtypeStruct((batch * n_chunks * tile, wv), out_dtype),
                   jax.ShapeDtypeStruct((batch, RET_HEADS, RET_DK, RET_DV), F32)],
        scratch_shapes=[pltpu.VMEM((RET_HEADS, RET_DK, RET_DV), F32)],
        compiler_params=_cparams(("parallel", "arbitrary")),
        name="retention_hp" if hp else "retention",
    )(rq, rk, rv, rg, cos2, sin2, decay, qdec, kdec, cdec, gn_g.reshape(1, wv).astype(F32), s0)


def _merge_kernel(osb_ref, ret_ref, wa_ref, wb_ref, ga_ref, gb_ref, m_ref, *scratch, hp):
    if hp:
        pa = _dot(osb_ref[...], wa_ref[...], True)
        pb = _dot(ret_ref[...], wb_ref[...], True)
    else:
        wa16_ref, wb16_ref = scratch

        @pl.when(pl.program_id(1) == 0)
        def _():
            wa16_ref[...] = wa_ref[...].astype(BF16)
            wb16_ref[...] = wb_ref[...].astype(BF16)

        pa = jnp.dot(osb_ref[...], wa16_ref[...], preferred_element_type=F32)
        pb = jnp.dot(ret_ref[...], wb16_ref[...], preferred_element_type=F32)
    m = jax.nn.sigmoid(ga_ref[...].astype(F32)) * pa + jax.nn.sigmoid(gb_ref[...].astype(F32)) * pb
    m_ref[...] = m.astype(m_ref.dtype)


def _merge(osb, ret, wa, wb, gab, *, hp, out_dtype, tm=512, tn=512):
    m, k = osb.shape
    n = wa.shape[1]
    tm = min(tm, m)
    nb = n // tn
    row = pl.BlockSpec((tm, k), lambda j, i: (i, 0))
    wsp = pl.BlockSpec((k, tn), lambda j, i: (0, j))
    tile = pl.BlockSpec((tm, tn), lambda j, i: (i, j))
    tile_b = pl.BlockSpec((tm, tn), lambda j, i: (i, nb + j))
    return pl.pallas_call(
        functools.partial(_merge_kernel, hp=hp),
        grid=(nb, m // tm),
        in_specs=[row, row, wsp, wsp, tile, tile_b],
        out_specs=tile,
        out_shape=jax.ShapeDtypeStruct((m, n), out_dtype),
        scratch_shapes=[] if hp else [pltpu.VMEM((k, tn), BF16), pltpu.VMEM((k, tn), BF16)],
        compiler_params=_cparams(("parallel", "arbitrary")),
        name="merge_hp" if hp else "merge",
    )(osb, ret, wa, wb, gab, gab)


def _layer_norm(x, g, b):
    mu = jnp.mean(x, axis=-1, keepdims=True)
    cen = x - mu
    var = jnp.mean(cen * cen, axis=-1, keepdims=True)
    return cen * lax.rsqrt(var + LN_EPS) * g + b


def _route(logits):
    lane = lax.broadcasted_iota(jnp.int32, logits.shape, 1)
    big = jnp.int32(ROUTE_LANES)
    neg = -jnp.inf

    def first_max(x):
        m = jnp.max(x, axis=1, keepdims=True)
        idx = jnp.min(jnp.where(x == m, lane, big), axis=1, keepdims=True)
        return m, idx

    gl = jnp.where(lane < N_GROUPS, logits, neg)
    gmax, gidx = first_max(gl)
    g_weight = 1.0 / jnp.sum(jnp.exp(gl - gmax), axis=1, keepdims=True)
    lo = N_GROUPS + gidx * EXPERTS_PER_GROUP
    el = jnp.where((lane >= lo) & (lane < lo + EXPERTS_PER_GROUP), logits, neg)
    v0, i0 = first_max(el)
    v1, i1 = first_max(jnp.where(lane == i0, neg, el))
    e = jnp.exp(v1 - v0)
    w0 = 1.0 / (1.0 + e)
    w1 = e / (1.0 + e)
    vals = [(i0 - N_GROUPS).astype(F32), (i1 - N_GROUPS).astype(F32), g_weight * w0, g_weight * w1]
    out = jnp.zeros(logits.shape, F32)
    for n, val in enumerate(vals):
        out = jnp.where(lane == n, val, out)
    return out


def _rows_to_tiles(x):
    t, w = x.shape
    return x.reshape(t, w // LANES, LANES)


def _tiles_to_rows(x3):
    x = jnp.swapaxes(x3, 0, 1)
    return jnp.concatenate([x[c] for c in range(x.shape[0])], axis=1)


def _out_ln_kernel(m_ref, wo_ref, x_ref, g_ref, b_ref, wr_ref, br_ref, h_ref, h3_ref, route_ref,
                   *scratch, hp, alpha, n_real):
    i = pl.program_id(0)
    if not hp:
        (wo16_ref,) = scratch

        @pl.when(i == 0)
        def _():
            wo16_ref[...] = wo_ref[...].astype(BF16)

    @pl.when(i < n_real)
    def _():
        if hp:
            mix = _dot(m_ref[...], wo_ref[...], True)
        else:
            mix = jnp.dot(m_ref[...], wo16_ref[...], preferred_element_type=F32)
        h = _layer_norm(alpha * x_ref[...] + mix, g_ref[...], b_ref[...])
        h_ref[...] = h
        h3_ref[...] = _rows_to_tiles(h.astype(BF16))
        logits = _dot(h, wr_ref[...], True) + br_ref[...]
        route_ref[...] = _route(logits)

    @pl.when(i >= n_real)
    def _():
        h3_ref[...] = jnp.zeros_like(h3_ref)


def _out_ln(m, w_out, x, ln_g, ln_b, w_route, b_route, *, hp, alpha, spare_tiles=0, tm=256):
    t, d = x.shape
    tm = min(tm, t)
    n_real = t // tm
    const = lambda i: (0, 0)
    row = lambda i: (jnp.minimum(i, n_real - 1), 0)
    return pl.pallas_call(
        functools.partial(_out_ln_kernel, hp=hp, alpha=alpha, n_real=n_real),
        grid=(n_real + spare_tiles,),
        in_specs=[pl.BlockSpec((tm, d), row),
                  pl.BlockSpec((d, d), const, pipeline_mode=pl.Buffered(1)),
                  pl.BlockSpec((tm, d), row),
                  pl.BlockSpec((1, d), const), pl.BlockSpec((1, d), const),
                  pl.BlockSpec((d, ROUTE_LANES), const), pl.BlockSpec((1, ROUTE_LANES), const)],
        out_specs=[pl.BlockSpec((tm, d), row),
                   pl.BlockSpec((tm, d // LANES, LANES), lambda i: (i, 0, 0)),
                   pl.BlockSpec((tm, ROUTE_LANES), row)],
        out_shape=[jax.ShapeDtypeStruct((t, d), F32),
                   jax.ShapeDtypeStruct((t + spare_tiles * tm, d // LANES, LANES), BF16),
                   jax.ShapeDtypeStruct((t, ROUTE_LANES), F32)],
        scratch_shapes=[] if hp else [pltpu.VMEM((d, d), BF16)],
        compiler_params=_cparams(("arbitrary",)),
        name="out_ln_hp" if hp else "out_ln",
    )(m, w_out, x, ln_g.reshape(1, d), ln_b.reshape(1, d), w_route, b_route)


def _row_copy(src_hbm, row, dst, dst_row, sem):
    return pltpu.make_async_copy(src_hbm.at[pl.ds(row, 1)], dst.at[pl.ds(dst_row, 1)], sem)


def _gather_rows(src_hbm, idx_ref, tile, n, dst, sem, wait):
    for r in range(n):
        cp = _row_copy(src_hbm, idx_ref[tile, r], dst, r, sem)
        if wait:
            cp.wait()
        else:
            cp.start()


def _moe_ffn_kernel(te_ref, nt_ref, idx_ref, h_hbm, w1_ref, w3_ref, w2_ref, y_ref, xbuf, sem):
    t = pl.program_id(0)
    n_tiles = nt_ref[0]
    slot = t % 2
    rows = xbuf.shape[1]
    nxt = jnp.minimum(t + 1, n_tiles - 1)

    @pl.when(t == 0)
    def _():
        _gather_rows(h_hbm, idx_ref, 0, rows, xbuf.at[0], sem.at[0], wait=False)

    @pl.when(t < n_tiles)
    def _():
        _gather_rows(h_hbm, idx_ref, t, rows, xbuf.at[slot], sem.at[slot], wait=True)
        x = _tiles_to_rows(xbuf[slot])
        _gather_rows(h_hbm, idx_ref, nxt, rows, xbuf.at[1 - slot], sem.at[1 - slot], wait=False)
        a = jnp.dot(x, w1_ref[...].astype(BF16), preferred_element_type=F32)
        c = jnp.dot(x, w3_ref[...].astype(BF16), preferred_element_type=F32)
        hid = (a * jax.nn.sigmoid(a)) * c
        y = jnp.dot(hid.astype(BF16), w2_ref[...].astype(BF16), preferred_element_type=F32)
        y_ref[...] = _rows_to_tiles(y.astype(BF16))

    @pl.when(t == n_tiles - 1)
    def _():
        _gather_rows(h_hbm, idx_ref, nxt, rows, xbuf.at[1 - slot], sem.at[1 - slot], wait=True)

    @pl.when(t >= n_tiles)
    def _():
        y_ref[...] = jnp.zeros_like(y_ref)


def _moe_ffn(h_all, tile_expert, n_tiles, slot_token, w1, w3, w2):
    nt = slot_token.shape[0]
    nc = h_all.shape[1]
    d = nc * LANES
    ff = w1.shape[2]
    grid_spec = pltpu.PrefetchScalarGridSpec(
        num_scalar_prefetch=3,
        grid=(nt,),
        in_specs=[
            pl.BlockSpec(memory_space=pl.ANY),
            pl.BlockSpec((None, d, ff), lambda t, te, nv, ix: (te[t], 0, 0)),
            pl.BlockSpec((None, d, ff), lambda t, te, nv, ix: (te[t], 0, 0)),
            pl.BlockSpec((None, ff, d), lambda t, te, nv, ix: (te[t], 0, 0)),
        ],
        out_specs=pl.BlockSpec((MOE_TILE, nc, LANES), lambda t, te, nv, ix: (t, 0, 0)),
        scratch_shapes=[pltpu.VMEM((2, MOE_TILE, nc, LANES), BF16), pltpu.SemaphoreType.DMA((2,))],
    )
    return pl.pallas_call(
        _moe_ffn_kernel,
        grid_spec=grid_spec,
        out_shape=jax.ShapeDtypeStruct((nt * MOE_TILE, nc, LANES), BF16),
        compiler_params=_cparams(("arbitrary",)),
        name="moe_ffn",
    )(tile_expert, n_tiles, slot_token, h_all, w1, w3, w2)


def _combine_ln_kernel(idx_ref, h_ref, route_ref, g_ref, b_ref, y_hbm, o_ref, ybuf, sem, *, alpha):
    t = pl.program_id(0)
    nt = pl.num_programs(0)
    slot = t % 2
    rows = ybuf.shape[1]
    nxt = jnp.minimum(t + 1, nt - 1)

    @pl.when(t == 0)
    def _():
        _gather_rows(y_hbm, idx_ref, 0, rows, ybuf.at[0], sem.at[0], wait=False)

    _gather_rows(y_hbm, idx_ref, t, rows, ybuf.at[slot], sem.at[slot], wait=True)
    tm = rows // 2
    ya = _tiles_to_rows(ybuf[slot, pl.ds(0, tm)]).astype(F32)
    yb = _tiles_to_rows(ybuf[slot, pl.ds(tm, tm)]).astype(F32)
    _gather_rows(y_hbm, idx_ref, nxt, rows, ybuf.at[1 - slot], sem.at[1 - slot], wait=False)
    route = route_ref[...]
    moe = route[:, 2:3] * ya + route[:, 3:4] * yb
    o_ref[...] = _layer_norm(alpha * h_ref[...] + moe, g_ref[...], b_ref[...])

    @pl.when(t == nt - 1)
    def _():
        _gather_rows(y_hbm, idx_ref, nxt, rows, ybuf.at[1 - slot], sem.at[1 - slot], wait=True)


def _combine_ln(h, n_rows, tok_slots, route, y_sorted, ln_g, ln_b, *, alpha, tm):
    d = h.shape[1]
    nc = y_sorted.shape[1]
    nt = n_rows // tm
    const = lambda t, ix: (0, 0)
    row = lambda t, ix: (t, 0)
    grid_spec = pltpu.PrefetchScalarGridSpec(
        num_scalar_prefetch=1,
        grid=(nt,),
        in_specs=[pl.BlockSpec((tm, d), row),
                  pl.BlockSpec((tm, ROUTE_LANES), row),
                  pl.BlockSpec((1, d), const), pl.BlockSpec((1, d), const),
                  pl.BlockSpec(memory_space=pl.ANY)],
        out_specs=pl.BlockSpec((tm, d), row),
        scratch_shapes=[pltpu.VMEM((2, 2 * tm, nc, LANES), BF16), pltpu.SemaphoreType.DMA((2,))],
    )
    return pl.pallas_call(
        functools.partial(_combine_ln_kernel, alpha=alpha),
        grid_spec=grid_spec,
        out_shape=jax.ShapeDtypeStruct((n_rows, d), F32),
        compiler_params=_cparams(("arbitrary",)),
        name="combine_ln",
    )(tok_slots, h, route, ln_g.reshape(1, d), ln_b.reshape(1, d), y_sorted)


def _moe_plan(route, n_tokens):
    e = jnp.concatenate([route[:, 0], route[:, 1]]).astype(jnp.int32)
    tok = jnp.concatenate([jnp.arange(n_tokens, dtype=jnp.int32)] * 2)
    n_assign = 2 * n_tokens
    nt = (n_assign + N_EXPERTS * (MOE_TILE - 1)) // MOE_TILE
    onehot = (e[:, None] == jnp.arange(N_EXPERTS, dtype=jnp.int32)[None, :]).astype(jnp.int32)
    csum = jnp.cumsum(onehot, axis=0)
    counts = csum[-1]
    rank = jnp.take_along_axis(csum, e[:, None], axis=1)[:, 0] - 1
    tiles_per_e = (counts + MOE_TILE - 1) // MOE_TILE
    tile_end = jnp.cumsum(tiles_per_e)
    tile_start = tile_end - tiles_per_e
    n_tiles = tile_end[-1:]
    slot = tile_start[e] * MOE_TILE + rank
    n_slots = nt * MOE_TILE
    slot_token = jnp.zeros((n_slots,), jnp.int32).at[slot].set(tok)
    tile_ids = jnp.minimum(jnp.arange(nt, dtype=jnp.int32), n_tiles[0] - 1)
    tile_expert = jnp.minimum(jnp.sum((tile_end[None, :] <= tile_ids[:, None]).astype(jnp.int32), axis=1),
                              N_EXPERTS - 1)
    return (tile_expert, n_tiles.astype(jnp.int32), slot_token.reshape(nt, MOE_TILE),
            slot[:n_tokens], slot[n_tokens:])


def _tile_slots(slot_a, slot_b, tm):
    n = slot_a.shape[0] // tm
    return jnp.concatenate([slot_a.reshape(n, tm), slot_b.reshape(n, tm)], axis=1)


def kernel(x_prompt, x_sample, cache_k, cache_v, state_ret, page_table, w_in, sb_bias, w_branch_a, w_branch_b,
           w_out, gn_g, ln1_g, ln1_b, ln2_g, ln2_b, w_group_router, b_group_router, w_expert_router,
           b_expert_router, w1, w3, w2):
    depth = w_in.shape[0]
    assert depth == 1, "single-layer step"
    batch, seq, d = x_prompt.shape
    db, tn, _ = x_sample.shape
    sb_w = SB_HEADS * SB_HEAD_DIM
    qk_w = RET_HEADS * RET_DK
    v_w = RET_HEADS * RET_DV
    widths = (sb_w, sb_w, sb_w, qk_w, qk_w, v_w, v_w, d, d)
    offs = np.concatenate([[0], np.cumsum(widths)])
    o_q, o_k, o_rq, o_rv, o_ga, o_end = offs[0], offs[1], offs[3], offs[5], offs[7], offs[9]
    alpha = float((2.0 * depth) ** 0.25)
    sb_scale = float(SB_HEAD_DIM ** -0.5) * LOG2_E
    n_pages = page_table.shape[1]
    page = cache_k.shape[2]
    past = n_pages * page

    w_in0 = w_in[0]
    wa, wb, wo = w_branch_a[0], w_branch_b[0], w_out[0]
    bias = sb_bias[0] * LOG2_E
    gn = gn_g[0]
    w_route = jnp.concatenate(
        [w_group_router[0], w_expert_router[0].reshape(d, N_EXPERTS),
         jnp.zeros((d, ROUTE_LANES - N_GROUPS - N_EXPERTS), F32)], axis=1)
    b_route = jnp.concatenate(
        [b_group_router[0], b_expert_router[0].reshape(N_EXPERTS),
         jnp.zeros((ROUTE_LANES - N_GROUPS - N_EXPERTS,), F32)]).reshape(1, ROUTE_LANES)

    t_p = batch * seq
    t_s = db * tn
    t_all = t_p + t_s

    xp = x_prompt.reshape(t_p, d)
    xp16 = xp.astype(BF16)
    (q16,) = _mm(xp16, w_in0, o_q, sb_w, (BF16,), scale=sb_scale)
    k32, k16 = _mm(xp16, w_in0, o_k, sb_w, (F32, BF16))
    v32, v16 = _mm(xp16, w_in0, o_k + sb_w, sb_w, (F32, BF16))
    (rqk,) = _mm(xp16, w_in0, o_rq, 2 * qk_w, (F32,))
    (rvg,) = _mm(xp16, w_in0, o_rv, 2 * v_w, (BF16,))
    (gab,) = _mm(xp16, w_in0, o_ga, 2 * d, (BF16,))

    o_sb = _sb_prompt(q16, k16, v16, bias, batch, seq)

    n_chunks = seq // RET_CHUNK
    tables_p = _ret_tables(np.arange(seq), RET_CHUNK, RET_CHUNK)
    s0_p = jnp.zeros((batch, RET_HEADS, RET_DK, RET_DV), F32)
    ret_p, st_p = _retention((rqk, 0), (rqk, 1), (rvg, 0), (rvg, 1), gn, s0_p, batch, n_chunks, tables_p,
                             hp=False, out_dtype=BF16)
    m_p = _merge(o_sb, ret_p, wa, wb, gab, hp=False, out_dtype=BF16)
    tm_h = 256
    assert t_s <= tm_h
    h_p, h3_all, route_p = _out_ln(m_p, wo, xp, ln1_g[0], ln1_b[0], w_route, b_route,
                                   hp=False, alpha=alpha, spare_tiles=1, tm=tm_h)

    xs = x_sample.reshape(t_s, d)
    (p_s,) = _mm(xs, w_in0, 0, int(o_end), (F32,), hp=True, tn=512)
    k_s = p_s[:, o_k:o_k + sb_w]
    v_s = p_s[:, o_k + sb_w:o_rq]
    q_s = p_s[:, o_q:o_k] * sb_scale
    o_sb_s = _sb_sample(q_s.reshape(db, tn, sb_w), k_s.reshape(db, tn, sb_w), v_s.reshape(db, tn, sb_w),
                        cache_k[0], cache_v[0], page_table, bias)

    def pad_rows(a):
        return jnp.pad(a.reshape(db, tn, -1), ((0, 0), (0, RET_CHUNK - tn), (0, 0))).reshape(db * RET_CHUNK, -1)

    pos_s = past + np.arange(RET_CHUNK)
    tables_s = _ret_tables(pos_s, tn, RET_CHUNK)
    rqk_s = pad_rows(p_s[:, o_rq:o_rv])
    rvg_s = pad_rows(p_s[:, o_rv:o_ga])
    ret_s_pad, st_s = _retention((rqk_s, 0), (rqk_s, 1), (rvg_s, 0), (rvg_s, 1), gn, state_ret[0], db, 1,
                                 tables_s, hp=True, out_dtype=F32)
    ret_s = ret_s_pad.reshape(db, RET_CHUNK, v_w)[:, :tn].reshape(t_s, v_w)
    m_s = _merge(o_sb_s.reshape(t_s, sb_w), ret_s, wa, wb, p_s[:, o_ga:o_end], hp=True, out_dtype=F32)
    h_s, h3_s, route_s = _out_ln(m_s, wo, xs, ln1_g[0], ln1_b[0], w_route, b_route, hp=True, alpha=alpha)

    h3_all = lax.dynamic_update_slice(h3_all, h3_s, (t_p, 0, 0))
    route = jnp.concatenate([route_p, route_s], axis=0)
    tile_expert, n_tiles, slot_token, slot_a, slot_b = _moe_plan(route, t_all)
    y_sorted = _moe_ffn(h3_all, tile_expert, n_tiles, slot_token, w1[0], w3[0], w2[0])
    tm_p = 256
    y_p = _combine_ln(h_p, t_p, _tile_slots(slot_a[:t_p], slot_b[:t_p], tm_p), route_p, y_sorted,
                      ln2_g[0], ln2_b[0], alpha=alpha, tm=tm_p)
    y_s = _combine_ln(h_s, t_s, _tile_slots(slot_a[t_p:], slot_b[t_p:], t_s), route_s, y_sorted,
                      ln2_g[0], ln2_b[0], alpha=alpha, tm=t_s)

    hs = (SB_HEADS, SB_HEAD_DIM)
    return (y_p.reshape(batch, seq, d), y_s.reshape(db, tn, d),
            k32.reshape(1, batch, seq, *hs), v32.reshape(1, batch, seq, *hs),
            st_p[None],
            k_s.reshape(1, db, tn, *hs), v_s.reshape(1, db, tn, *hs),
            st_s[None])
```

```python
import functools

import numpy as np
import jax
import jax.numpy as jnp
from jax import lax
from jax.experimental import pallas as pl
from jax.experimental.pallas import tpu as pltpu

F32 = jnp.float32
BF16 = jnp.bfloat16

SB_HEADS = 16
SB_HEAD_DIM = 128
RET_HEADS = 8
RET_DK = 128
RET_DV = 256
RET_CHUNK = 128
ROPE_BASE = 10000.0
N_GROUPS = 4
EXPERTS_PER_GROUP = 8
N_EXPERTS = N_GROUPS * EXPERTS_PER_GROUP
LN_EPS = 1e-5
GN_EPS = 1e-5

V7X_VMEM_LIMIT_BYTES = 56 * 1024 * 1024
LANES = 128
MOE_TILE = 256
ROUTE_LANES = LANES


def _cparams(sem):
    return pltpu.CompilerParams(dimension_semantics=sem, vmem_limit_bytes=V7X_VMEM_LIMIT_BYTES)


def _split_hi_lo(a):
    hi = a.astype(BF16)
    lo = (a - hi.astype(F32)).astype(BF16)
    return hi, lo


def _dot(a, b, hp, dims=(((1,), (0,)), ((), ()))):
    if not hp:
        return lax.dot_general(a.astype(BF16), b.astype(BF16), dims, preferred_element_type=F32)
    ah, al = _split_hi_lo(a.astype(F32))
    bh, bl = _split_hi_lo(b.astype(F32))
    d = functools.partial(lax.dot_general, dimension_numbers=dims, preferred_element_type=F32)
    return d(ah, bh) + (d(ah, bl) + d(al, bh))


_NT = (((1,), (1,)), ((), ()))
_TN = (((0,), (0,)), ((), ()))


def _mm_kernel(x_ref, w_ref, *refs, hp, scale):
    if hp:
        out_refs = refs
        acc = _dot(x_ref[...], w_ref[...], True)
    else:
        *out_refs, w16_ref = refs

        @pl.when(pl.program_id(1) == 0)
        def _():
            w16_ref[...] = w_ref[...].astype(BF16)

        acc = jnp.dot(x_ref[...], w16_ref[...], preferred_element_type=F32)
    if scale is not None:
        acc = acc * scale
    for o in out_refs:
        o[...] = acc.astype(o.dtype)


def _mm(x, w, col0, n, out_dtypes, *, hp=False, scale=None, tm=512, tn=1024):
    m, k = x.shape
    col0, n = int(col0), int(n)
    tm = min(tm, m)
    tn = min(tn, n)
    assert m % tm == 0 and n % tn == 0 and col0 % tn == 0
    j0 = col0 // tn
    outs = pl.pallas_call(
        functools.partial(_mm_kernel, hp=hp, scale=scale),
        grid=(n // tn, m // tm),
        in_specs=[pl.BlockSpec((tm, k), lambda j, i: (i, 0)),
                  pl.BlockSpec((k, tn), lambda j, i: (0, j0 + j))],
        out_specs=[pl.BlockSpec((tm, tn), lambda j, i: (i, j)) for _ in out_dtypes],
        out_shape=[jax.ShapeDtypeStruct((m, n), dt) for dt in out_dtypes],
        scratch_shapes=[] if hp else [pltpu.VMEM((k, tn), BF16)],
        compiler_params=_cparams(("parallel", "arbitrary")),
        name="proj_hp" if hp else "proj",
    )(x, w)
    return outs


LOG2_E = 1.4426950408889634


EXP2_MAX_ARG = 126.0


def _softplus2(z2):
    return jnp.maximum(jnp.log2(1.0 + jnp.exp2(jnp.minimum(z2, EXP2_MAX_ARG))), z2)


def _strict_lower_ones(n):
    r = lax.broadcasted_iota(jnp.int32, (n, n), 0)
    c = lax.broadcasted_iota(jnp.int32, (n, n), 1)
    return jnp.where(r > c, 1.0, 0.0).astype(BF16)


SB_SUB = 256


def _sb_logit_stage(z2, valid, sub):
    s = _softplus2(z2)
    zs = z2 - s
    if valid is not None:
        s = jnp.where(valid, s, 0.0)
        zs = jnp.where(valid, zs, -jnp.inf)
    width = z2.shape[1]
    sub = min(sub, width)
    tri = _strict_lower_ones(sub)
    tails = []
    run = None
    for c in reversed(range(width // sub)):
        sc = s[:, c * sub:(c + 1) * sub]
        t = jnp.dot(sc.astype(BF16), tri, preferred_element_type=F32)
        rs = jnp.sum(sc, axis=1, keepdims=True)
        if run is None:
            run = rs
        else:
            t = t + run
            run = run + rs
        tails.append(t)
    return zs, jnp.concatenate(tails[::-1], axis=1), run


def _sb_weights(zs, tail):
    return jnp.exp2(zs - tail).astype(BF16)


def _sb_prompt_kernel(bias_ref, q_ref, k_ref, v_ref, o_ref, acc_ref, zs_ref, tail_ref, carry_ref,
                      rowsum_ref, *, tq):
    h = pl.program_id(1)
    qi = pl.program_id(2)
    lane = lax.broadcasted_iota(jnp.int32, (tq, SB_HEAD_DIM), 1)
    b = jnp.full((tq, SB_HEAD_DIM), bias_ref[h], F32)
    p0 = b.astype(BF16).astype(F32)
    p1 = (b - p0).astype(BF16).astype(F32)
    p2 = ((b - p0) - p1).astype(BF16).astype(F32)
    q_bias = jnp.where(lane == 0, p0, jnp.where(lane == 1, p1, jnp.where(lane == 2, p2, 0.0))).astype(BF16)
    k_ones = jnp.where(lane < 3, 1.0, 0.0).astype(BF16)
    q = jnp.concatenate([q_ref[...], q_bias], axis=1)
    acc_ref[...] = jnp.zeros_like(acc_ref)
    carry_ref[...] = jnp.zeros_like(carry_ref)

    def rows(j):
        return pl.ds(pl.multiple_of(j * tq, tq), tq)

    def logits(j, masked, p):
        k = jnp.concatenate([k_ref[rows(j), :], k_ones], axis=1)
        z2 = lax.dot_general(q, k, _NT, preferred_element_type=F32)
        valid = None
        if masked:
            valid = (lax.broadcasted_iota(jnp.int32, (tq, tq), 1)
                     < lax.broadcasted_iota(jnp.int32, (tq, tq), 0))
        zs_ref[p], tail_ref[p], rowsum_ref[p] = _sb_logit_stage(z2, valid, SB_SUB)

    def weights(j, p):
        w = _sb_weights(zs_ref[p], tail_ref[p])
        carry = carry_ref[...]
        acc_ref[...] += jnp.dot(w, v_ref[rows(j), :], preferred_element_type=F32) * jnp.exp2(-carry)
        carry_ref[...] = carry + rowsum_ref[p]

    def step(m, p):
        weights(qi - m, p)
        logits(qi - m - 1, False, 1 - p)

    logits(qi, True, 0)

    def pair(t, c):
        step(2 * t, 0)
        step(2 * t + 1, 1)
        return c

    lax.fori_loop(0, qi // 2, pair, 0)

    @pl.when(qi % 2 == 1)
    def _():
        step(qi - 1, 0)

    weights(0, qi % 2)
    o_ref[...] = acc_ref[...].astype(o_ref.dtype)


def _sb_prompt(q, k, v, sb_bias, batch, seq, *, tq=1024):
    tq = min(tq, seq)
    nq = seq // tq
    d = SB_HEAD_DIM
    return pl.pallas_call(
        functools.partial(_sb_prompt_kernel, tq=tq),
        grid=(batch, SB_HEADS, nq),
        in_specs=[pl.BlockSpec(memory_space=pltpu.SMEM),
                  pl.BlockSpec((tq, d), lambda b, h, i: (b * nq + i, h)),
                  pl.BlockSpec((seq, d), lambda b, h, i: (b, h)),
                  pl.BlockSpec((seq, d), lambda b, h, i: (b, h))],
        out_specs=pl.BlockSpec((tq, d), lambda b, h, i: (b * nq + i, h)),
        out_shape=jax.ShapeDtypeStruct(q.shape, BF16),
        scratch_shapes=[pltpu.VMEM((tq, d), F32), pltpu.VMEM((2, tq, tq), F32), pltpu.VMEM((2, tq, tq), F32),
                        pltpu.VMEM((tq, 1), F32), pltpu.VMEM((2, tq, 1), F32)],
        compiler_params=_cparams(("parallel", "parallel", "arbitrary")),
        name="sb_prompt",
    )(sb_bias.astype(F32), q, k, v)


SB_SAMPLE_PAGES = 4


def _sb_sample_kernel(pt_ref, qbd_ref, brow_ref, expand_ref, hmask_ref, knew_ref, vnew_ref, *refs,
                      n_groups, dec_seq):
    npg = SB_SAMPLE_PAGES
    k_refs, v_refs = refs[:npg], refs[npg:2 * npg]
    o_ref, acc_ref, carry_ref = refs[2 * npg:]
    g = pl.program_id(1)
    qbd = qbd_ref[...]
    n_rows = qbd.shape[0]
    page = expand_ref.shape[0]

    def attend(k_all, v_rows, valid):
        z2 = lax.dot_general(qbd, k_all, _NT, preferred_element_type=F32) + brow_ref[...]
        zs, tail, rowsum = _sb_logit_stage(z2, valid, SB_SUB)
        w = _sb_weights(zs, tail)
        spread = [jnp.dot(w[:, p * page:(p + 1) * page], expand_ref[...], preferred_element_type=F32)
                  * hmask_ref[...] for p in range(w.shape[1] // page)]
        w_rows = jnp.concatenate(spread, axis=1).astype(BF16)
        carry = carry_ref[...]
        acc_ref[...] += jnp.dot(w_rows, v_rows, preferred_element_type=F32) * jnp.exp2(-carry)
        carry_ref[...] = carry + rowsum

    @pl.when(g == 0)
    def _():
        acc_ref[...] = jnp.zeros_like(acc_ref)
        carry_ref[...] = jnp.zeros_like(carry_ref)
        t = lax.broadcasted_iota(jnp.int32, (n_rows, page), 0) % dec_seq
        s = lax.broadcasted_iota(jnp.int32, (n_rows, page), 1)
        attend(knew_ref[...], vnew_ref[...], s < t)

    @pl.when(g > 0)
    def _():
        def k_page(ref):
            x = jnp.swapaxes(ref[...].astype(BF16), 0, 1)
            return jnp.concatenate([x[h] for h in range(SB_HEADS)], axis=1)

        k_all = jnp.concatenate([k_page(r) for r in k_refs], axis=0)
        v_rows = jnp.concatenate([r[...].astype(BF16) for r in v_refs], axis=0)
        attend(k_all, v_rows, None)

    @pl.when(g == n_groups)
    def _():
        o_ref[...] = acc_ref[...]


def _sb_sample(q_s, k_new, v_new, cache_k, cache_v, page_table, sb_bias):
    db, tn, w = q_s.shape
    n_pages = page_table.shape[1]
    n_pool, page = cache_k.shape[:2]
    nh, d = SB_HEADS, SB_HEAD_DIM
    npg = SB_SAMPLE_PAGES
    assert n_pages % npg == 0 and tn <= page
    n_groups = n_pages // npg
    n_rows = nh * tn
    qh = q_s.reshape(db, tn, nh, d).transpose(0, 2, 1, 3)
    eye = jnp.eye(nh, dtype=F32)
    qbd = (qh[:, :, :, None, :] * eye[None, :, None, :, None]).reshape(db, n_rows, w).astype(BF16)
    brow = jnp.repeat(sb_bias.astype(F32), tn).reshape(n_rows, 1)
    expand = jnp.asarray(np.repeat(np.eye(page, dtype=np.float32), nh, axis=1), BF16)
    hmask = jnp.asarray(np.tile(np.repeat(np.eye(nh, dtype=np.float32), tn, axis=0), (1, page)))
    pad = ((0, 0), (0, page - tn), (0, 0))
    knew = jnp.pad(k_new, pad).astype(BF16)
    vnew = jnp.pad(v_new, pad).astype(BF16).reshape(db, page * nh, d)
    v_rows = cache_v.reshape(n_pool, page * nh, d)

    def cache_map(p):
        return lambda b, g, pt: (pt[b * n_pages + n_pages - jnp.maximum(g, 1) * npg + p], 0, 0, 0)

    def cache_map_v(p):
        return lambda b, g, pt: (pt[b * n_pages + n_pages - jnp.maximum(g, 1) * npg + p], 0, 0)

    per_b = lambda b, g, pt: (b, 0, 0)
    const = lambda b, g, pt: (0, 0)
    grid_spec = pltpu.PrefetchScalarGridSpec(
        num_scalar_prefetch=1,
        grid=(db, n_groups + 1),
        in_specs=[pl.BlockSpec((None, n_rows, w), per_b),
                  pl.BlockSpec((n_rows, 1), const),
                  pl.BlockSpec((page, page * nh), const),
                  pl.BlockSpec((n_rows, page * nh), const),
                  pl.BlockSpec((None, page, w), per_b),
                  pl.BlockSpec((None, page * nh, d), per_b)]
        + [pl.BlockSpec((None, page, nh, d), cache_map(p)) for p in range(npg)]
        + [pl.BlockSpec((None, page * nh, d), cache_map_v(p)) for p in range(npg)],
        out_specs=pl.BlockSpec((None, n_rows, d), per_b),
        scratch_shapes=[pltpu.VMEM((n_rows, d), F32), pltpu.VMEM((n_rows, 1), F32)],
    )
    out = pl.pallas_call(
        functools.partial(_sb_sample_kernel, n_groups=n_groups, dec_seq=tn),
        grid_spec=grid_spec,
        out_shape=jax.ShapeDtypeStruct((db, n_rows, d), F32),
        compiler_params=_cparams(("parallel", "arbitrary")),
        name="sb_sample",
    )(page_table.reshape(-1).astype(jnp.int32), qbd, brow, expand, hmask, knew, vnew,
      *([cache_k] * npg), *([v_rows] * npg))
    return out.reshape(db, nh, tn, d).transpose(0, 2, 1, 3).reshape(db, tn, w)


def _ret_tables(positions, chunk_len, tile):
    half = RET_DK // 2
    inv = np.float32(ROPE_BASE) ** (-(np.arange(half, dtype=np.float32) / np.float32(half)))
    ang = positions.astype(np.float32)[:, None] * inv.astype(np.float32)[None, :]
    c = np.cos(ang.astype(np.float64)).astype(np.float32)
    s = np.sin(ang.astype(np.float64)).astype(np.float32)
    cos2 = np.concatenate([c, c], axis=1)
    sin2 = np.concatenate([-s, s], axis=1)
    hh = np.arange(RET_HEADS, dtype=np.float32)
    log_g = np.log(np.float32(1.0) - np.float32(2.0) ** (np.float32(-5.0) - hh)).astype(np.float32)
    i = np.arange(tile, dtype=np.float32)
    live = (np.arange(tile) < chunk_len)
    diff = i[:, None] - i[None, :]
    decay = np.where((diff >= 0)[None] & live[None, :, None] & live[None, None, :],
                     np.exp(np.maximum(diff, 0.0)[None] * log_g[:, None, None]), 0.0).astype(np.float32)
    q_dec = np.where(live[None], np.exp((i + 1.0)[None] * log_g[:, None]), 0.0).astype(np.float32)
    k_dec = np.where(live[None], np.exp((np.float32(chunk_len) - 1.0 - i)[None] * log_g[:, None]),
                     0.0).astype(np.float32)
    c_dec = np.exp(np.float32(chunk_len) * log_g).astype(np.float32)
    k_scale = np.float32(RET_DK) ** np.float32(-0.5)
    qdec_full = np.repeat(q_dec.T, RET_DK, axis=1)
    kdec_full = np.repeat(k_dec.T, RET_DK, axis=1)
    cdec_full = np.broadcast_to(c_dec[:, None, None], (RET_HEADS, 1, RET_DV)).copy()
    return (jnp.asarray(cos2), jnp.asarray(sin2), jnp.asarray(decay), jnp.asarray(qdec_full),
            jnp.asarray(kdec_full), jnp.asarray(cdec_full), float(k_scale))


def _retention_kernel(rq_ref, rk_ref, rv_ref, rg_ref, cos_ref, sin_ref, decay_ref, qdec_ref, kdec_ref,
                      cdec_ref, gn_ref, s0_ref, ret_ref, sfin_ref, state_ref, *, hp, k_scale):
    c = pl.program_id(1)

    @pl.when(c == 0)
    def _():
        state_ref[...] = s0_ref[...]

    cos = cos_ref[...]
    sin = sin_ref[...]

    def rope(x):
        return x * cos + pltpu.roll(x, RET_DK // 2, 1) * sin

    for h in range(RET_HEADS):
        ks = slice(h * RET_DK, (h + 1) * RET_DK)
        vs = slice(h * RET_DV, (h + 1) * RET_DV)
        q = rope(rq_ref[:, ks].astype(F32))
        k = rope(rk_ref[:, ks].astype(F32)) * k_scale
        v = rv_ref[:, vs]
        scores = _dot(q, k, hp, _NT) * decay_ref[h]
        intra = _dot(scores, v, hp)
        state = state_ref[h]
        inter = _dot(q * qdec_ref[:, ks], state, hp)
        state_ref[h] = state * cdec_ref[h] + _dot(k * kdec_ref[:, ks], v, hp, _TN)
        o = intra + inter
        mu = jnp.mean(o, axis=-1, keepdims=True)
        cen = o - mu
        var = jnp.mean(cen * cen, axis=-1, keepdims=True)
        g = rg_ref[:, vs].astype(F32)
        ret = cen * lax.rsqrt(var + GN_EPS) * gn_ref[:, vs] * (g * jax.nn.sigmoid(g))
        ret_ref[:, vs] = ret.astype(ret_ref.dtype)

    @pl.when(c == pl.num_programs(1) - 1)
    def _():
        sfin_ref[...] = state_ref[...]


def _retention(rq, rk, rv, rg, gn_g, s0, batch, n_chunks, tables, *, hp, out_dtype):
    cos2, sin2, decay, qdec, kdec, cdec, k_scale = tables
    tile = decay.shape[1]
    wk = RET_HEADS * RET_DK
    wv = RET_HEADS * RET_DV
    (rq, cq), (rk, ck), (rv, cv), (rg, cg) = rq, rk, rv, rg
    row = lambda b, c: (b * n_chunks + c, 0)
    rowc = lambda cb: (lambda b, c: (b * n_chunks + c, cb))
    const2 = lambda b, c: (0, 0)
    const3 = lambda b, c: (0, 0, 0)
    return pl.pallas_call(
        functools.partial(_retention_kernel, hp=hp, k_scale=k_scale),
        grid=(batch, n_chunks),
        in_specs=[pl.BlockSpec((tile, wk), rowc(cq)), pl.BlockSpec((tile, wk), rowc(ck)),
                  pl.BlockSpec((tile, wv), rowc(cv)), pl.BlockSpec((tile, wv), rowc(cg)),
                  pl.BlockSpec((tile, RET_DK), lambda b, c: (c, 0)),
                  pl.BlockSpec((tile, RET_DK), lambda b, c: (c, 0)),
                  pl.BlockSpec(decay.shape, const3),
                  pl.BlockSpec(qdec.shape, const2), pl.BlockSpec(kdec.shape, const2),
                  pl.BlockSpec(cdec.shape, const3),
                  pl.BlockSpec((1, wv), const2),
                  pl.BlockSpec((None, RET_HEADS, RET_DK, RET_DV), lambda b, c: (b, 0, 0, 0))],
        out_specs=[pl.BlockSpec((tile, wv), row),
                   pl.BlockSpec((None, RET_HEADS, RET_DK, RET_DV), lambda b, c: (b, 0, 0, 0))],
        out_shape=[jax.ShapeDtypeStruct((batch * n_chunks * tile, wv), out_dtype),
                   jax.ShapeDtypeStruct((batch, RET_HEADS, RET_DK, RET_DV), F32)],
        scratch_shapes=[pltpu.VMEM((RET_HEADS, RET_DK, RET_DV), F32)],
        compiler_params=_cparams(("parallel", "arbitrary")),
        name="retention_hp" if hp else "retention",
    )(rq, rk, rv, rg, cos2, sin2, decay, qdec, kdec, cdec, gn_g.reshape(1, wv).astype(F32), s0)


def _merge_kernel(osb_ref, ret_ref, wa_ref, wb_ref, ga_ref, gb_ref, m_ref, *scratch, hp):
    if hp:
        pa = _dot(osb_ref[...], wa_ref[...], True)
        pb = _dot(ret_ref[...], wb_ref[...], True)
    else:
        wa16_ref, wb16_ref = scratch

        @pl.when(pl.program_id(1) == 0)
        def _():
            wa16_ref[...] = wa_ref[...].astype(BF16)
            wb16_ref[...] = wb_ref[...].astype(BF16)

        pa = jnp.dot(osb_ref[...], wa16_ref[...], preferred_element_type=F32)
        pb = jnp.dot(ret_ref[...], wb16_ref[...], preferred_element_type=F32)
    m = jax.nn.sigmoid(ga_ref[...].astype(F32)) * pa + jax.nn.sigmoid(gb_ref[...].astype(F32)) * pb
    m_ref[...] = m.astype(m_ref.dtype)


def _merge(osb, ret, wa, wb, gab, *, hp, out_dtype, tm=512, tn=512):
    m, k = osb.shape
    n = wa.shape[1]
    tm = min(tm, m)
    nb = n // tn
    row = pl.BlockSpec((tm, k), lambda j, i: (i, 0))
    wsp = pl.BlockSpec((k, tn), lambda j, i: (0, j))
    tile = pl.BlockSpec((tm, tn), lambda j, i: (i, j))
    tile_b = pl.BlockSpec((tm, tn), lambda j, i: (i, nb + j))
    return pl.pallas_call(
        functools.partial(_merge_kernel, hp=hp),
        grid=(nb, m // tm),
        in_specs=[row, row, wsp, wsp, tile, tile_b],
        out_specs=tile,
        out_shape=jax.ShapeDtypeStruct((m, n), out_dtype),
        scratch_shapes=[] if hp else [pltpu.VMEM((k, tn), BF16), pltpu.VMEM((k, tn), BF16)],
        compiler_params=_cparams(("parallel", "arbitrary")),
        name="merge_hp" if hp else "merge",
    )(osb, ret, wa, wb, gab, gab)


def _layer_norm(x, g, b):
    mu = jnp.mean(x, axis=-1, keepdims=True)
    cen = x - mu
    var = jnp.mean(cen * cen, axis=-1, keepdims=True)
    return cen * lax.rsqrt(var + LN_EPS) * g + b


def _route(logits):
    lane = lax.broadcasted_iota(jnp.int32, logits.shape, 1)
    big = jnp.int32(ROUTE_LANES)
    neg = -jnp.inf

    def first_max(x):
        m = jnp.max(x, axis=1, keepdims=True)
        idx = jnp.min(jnp.where(x == m, lane, big), axis=1, keepdims=True)
        return m, idx

    gl = jnp.where(lane < N_GROUPS, logits, neg)
    gmax, gidx = first_max(gl)
    g_weight = 1.0 / jnp.sum(jnp.exp(gl - gmax), axis=1, keepdims=True)
    lo = N_GROUPS + gidx * EXPERTS_PER_GROUP
    el = jnp.where((lane >= lo) & (lane < lo + EXPERTS_PER_GROUP), logits, neg)
    v0, i0 = first_max(el)
    v1, i1 = first_max(jnp.where(lane == i0, neg, el))
    e = jnp.exp(v1 - v0)
    w0 = 1.0 / (1.0 + e)
    w1 = e / (1.0 + e)
    vals = [(i0 - N_GROUPS).astype(F32), (i1 - N_GROUPS).astype(F32), g_weight * w0, g_weight * w1]
    out = jnp.zeros(logits.shape, F32)
    for n, val in enumerate(vals):
        out = jnp.where(lane == n, val, out)
    return out


def _rows_to_tiles(x):
    t, w = x.shape
    return x.reshape(t, w // LANES, LANES)


def _tiles_to_rows(x3):
    x = jnp.swapaxes(x3, 0, 1)
    return jnp.concatenate([x[c] for c in range(x.shape[0])], axis=1)


def _out_ln_kernel(m_ref, wo_ref, x_ref, g_ref, b_ref, wr_ref, br_ref, h_ref, h3_ref, route_ref,
                   *scratch, hp, alpha, n_real):
    i = pl.program_id(0)
    if not hp:
        (wo16_ref,) = scratch

        @pl.when(i == 0)
        def _():
            wo16_ref[...] = wo_ref[...].astype(BF16)

    @pl.when(i < n_real)
    def _():
        if hp:
            mix = _dot(m_ref[...], wo_ref[...], True)
        else:
            mix = jnp.dot(m_ref[...], wo16_ref[...], preferred_element_type=F32)
        h = _layer_norm(alpha * x_ref[...] + mix, g_ref[...], b_ref[...])
        h_ref[...] = h
        h3_ref[...] = _rows_to_tiles(h.astype(BF16))
        logits = _dot(h, wr_ref[...], True) + br_ref[...]
        route_ref[...] = _route(logits)

    @pl.when(i >= n_real)
    def _():
        h3_ref[...] = jnp.zeros_like(h3_ref)


def _out_ln(m, w_out, x, ln_g, ln_b, w_route, b_route, *, hp, alpha, spare_tiles=0, tm=256):
    t, d = x.shape
    tm = min(tm, t)
    n_real = t // tm
    const = lambda i: (0, 0)
    row = lambda i: (jnp.minimum(i, n_real - 1), 0)
    return pl.pallas_call(
        functools.partial(_out_ln_kernel, hp=hp, alpha=alpha, n_real=n_real),
        grid=(n_real + spare_tiles,),
        in_specs=[pl.BlockSpec((tm, d), row),
                  pl.BlockSpec((d, d), const, pipeline_mode=pl.Buffered(1)),
                  pl.BlockSpec((tm, d), row),
                  pl.BlockSpec((1, d), const), pl.BlockSpec((1, d), const),
                  pl.BlockSpec((d, ROUTE_LANES), const), pl.BlockSpec((1, ROUTE_LANES), const)],
        out_specs=[pl.BlockSpec((tm, d), row),
                   pl.BlockSpec((tm, d // LANES, LANES), lambda i: (i, 0, 0)),
                   pl.BlockSpec((tm, ROUTE_LANES), row)],
        out_shape=[jax.ShapeDtypeStruct((t, d), F32),
                   jax.ShapeDtypeStruct((t + spare_tiles * tm, d // LANES, LANES), BF16),
                   jax.ShapeDtypeStruct((t, ROUTE_LANES), F32)],
        scratch_shapes=[] if hp else [pltpu.VMEM((d, d), BF16)],
        compiler_params=_cparams(("arbitrary",)),
        name="out_ln_hp" if hp else "out_ln",
    )(m, w_out, x, ln_g.reshape(1, d), ln_b.reshape(1, d), w_route, b_route)


def _row_copy(src_hbm, row, dst, dst_row, sem):
    return pltpu.make_async_copy(src_hbm.at[pl.ds(row, 1)], dst.at[pl.ds(dst_row, 1)], sem)


def _gather_rows(src_hbm, idx_ref, tile, n, dst, sem, wait):
    for r in range(n):
        cp = _row_copy(src_hbm, idx_ref[tile, r], dst, r, sem)
        if wait:
            cp.wait()
        else:
            cp.start(priority=r % 2)


def _moe_ffn_kernel(te_ref, nt_ref, idx_ref, h_hbm, w1_ref, w3_ref, w2_ref, y_ref, xbuf, sem):
    t = pl.program_id(0)
    n_tiles = nt_ref[0]
    slot = t % 2
    rows = xbuf.shape[1]
    nxt = jnp.minimum(t + 1, n_tiles - 1)

    @pl.when(t == 0)
    def _():
        _gather_rows(h_hbm, idx_ref, 0, rows, xbuf.at[0], sem.at[0], wait=False)

    @pl.when(t < n_tiles)
    def _():
        _gather_rows(h_hbm, idx_ref, nxt, rows, xbuf.at[1 - slot], sem.at[1 - slot], wait=False)
        _gather_rows(h_hbm, idx_ref, t, rows, xbuf.at[slot], sem.at[slot], wait=True)
        x = _tiles_to_rows(xbuf[slot])
        a = jnp.dot(x, w1_ref[...].astype(BF16), preferred_element_type=F32)
        c = jnp.dot(x, w3_ref[...].astype(BF16), preferred_element_type=F32)
        hid = (a * jax.nn.sigmoid(a)) * c
        y = jnp.dot(hid.astype(BF16), w2_ref[...].astype(BF16), preferred_element_type=F32)
        y_ref[...] = _rows_to_tiles(y.astype(BF16))

    @pl.when(t == n_tiles - 1)
    def _():
        _gather_rows(h_hbm, idx_ref, nxt, rows, xbuf.at[1 - slot], sem.at[1 - slot], wait=True)

    @pl.when(t >= n_tiles)
    def _():
        y_ref[...] = jnp.zeros_like(y_ref)


def _moe_ffn(h_all, tile_expert, n_tiles, slot_token, w1, w3, w2):
    nt = slot_token.shape[0]
    nc = h_all.shape[1]
    d = nc * LANES
    ff = w1.shape[2]
    grid_spec = pltpu.PrefetchScalarGridSpec(
        num_scalar_prefetch=3,
        grid=(nt,),
        in_specs=[
            pl.BlockSpec(memory_space=pl.ANY),
            pl.BlockSpec((None, d, ff), lambda t, te, nv, ix: (te[t], 0, 0)),
            pl.BlockSpec((None, d, ff), lambda t, te, nv, ix: (te[t], 0, 0)),
            pl.BlockSpec((None, ff, d), lambda t, te, nv, ix: (te[t], 0, 0)),
        ],
        out_specs=pl.BlockSpec((MOE_TILE, nc, LANES), lambda t, te, nv, ix: (t, 0, 0)),
        scratch_shapes=[pltpu.VMEM((2, MOE_TILE, nc, LANES), BF16), pltpu.SemaphoreType.DMA((2,))],
    )
    return pl.pallas_call(
        _moe_ffn_kernel,
        grid_spec=grid_spec,
        out_shape=jax.ShapeDtypeStruct((nt * MOE_TILE, nc, LANES), BF16),
        compiler_params=_cparams(("arbitrary",)),
        name="moe_ffn",
    )(tile_expert, n_tiles, slot_token, h_all, w1, w3, w2)


def _combine_ln_kernel(idx_ref, h_ref, route_ref, g_ref, b_ref, y_hbm, o_ref, ybuf, sem, *, alpha):
    t = pl.program_id(0)
    nt = pl.num_programs(0)
    slot = t % 2
    rows = ybuf.shape[1]
    nxt = jnp.minimum(t + 1, nt - 1)

    @pl.when(t == 0)
    def _():
        _gather_rows(y_hbm, idx_ref, 0, rows, ybuf.at[0], sem.at[0], wait=False)

    _gather_rows(y_hbm, idx_ref, t, rows, ybuf.at[slot], sem.at[slot], wait=True)
    tm = rows // 2
    ya = _tiles_to_rows(ybuf[slot, pl.ds(0, tm)]).astype(F32)
    yb = _tiles_to_rows(ybuf[slot, pl.ds(tm, tm)]).astype(F32)
    _gather_rows(y_hbm, idx_ref, nxt, rows, ybuf.at[1 - slot], sem.at[1 - slot], wait=False)
    route = route_ref[...]
    moe = route[:, 2:3] * ya + route[:, 3:4] * yb
    o_ref[...] = _layer_norm(alpha * h_ref[...] + moe, g_ref[...], b_ref[...])

    @pl.when(t == nt - 1)
    def _():
        _gather_rows(y_hbm, idx_ref, nxt, rows, ybuf.at[1 - slot], sem.at[1 - slot], wait=True)


def _combine_ln(h, n_rows, tok_slots, route, y_sorted, ln_g, ln_b, *, alpha, tm):
    d = h.shape[1]
    nc = y_sorted.shape[1]
    nt = n_rows // tm
    const = lambda t, ix: (0, 0)
    row = lambda t, ix: (t, 0)
    grid_spec = pltpu.PrefetchScalarGridSpec(
        num_scalar_prefetch=1,
        grid=(nt,),
        in_specs=[pl.BlockSpec((tm, d), row),
                  pl.BlockSpec((tm, ROUTE_LANES), row),
                  pl.BlockSpec((1, d), const), pl.BlockSpec((1, d), const),
                  pl.BlockSpec(memory_space=pl.ANY)],
        out_specs=pl.BlockSpec((tm, d), row),
        scratch_shapes=[pltpu.VMEM((2, 2 * tm, nc, LANES), BF16), pltpu.SemaphoreType.DMA((2,))],
    )
    return pl.pallas_call(
        functools.partial(_combine_ln_kernel, alpha=alpha),
        grid_spec=grid_spec,
        out_shape=jax.ShapeDtypeStruct((n_rows, d), F32),
        compiler_params=_cparams(("arbitrary",)),
        name="combine_ln",
    )(tok_slots, h, route, ln_g.reshape(1, d), ln_b.reshape(1, d), y_sorted)


def _moe_plan(route, n_tokens):
    e = jnp.concatenate([route[:, 0], route[:, 1]]).astype(jnp.int32)
    tok = jnp.concatenate([jnp.arange(n_tokens, dtype=jnp.int32)] * 2)
    n_assign = 2 * n_tokens
    nt = (n_assign + N_EXPERTS * (MOE_TILE - 1)) // MOE_TILE
    onehot = (e[:, None] == jnp.arange(N_EXPERTS, dtype=jnp.int32)[None, :]).astype(jnp.int32)
    csum = jnp.cumsum(onehot, axis=0)
    counts = csum[-1]
    rank = jnp.take_along_axis(csum, e[:, None], axis=1)[:, 0] - 1
    tiles_per_e = (counts + MOE_TILE - 1) // MOE_TILE
    tile_end = jnp.cumsum(tiles_per_e)
    tile_start = tile_end - tiles_per_e
    n_tiles = tile_end[-1:]
    slot = tile_start[e] * MOE_TILE + rank
    n_slots = nt * MOE_TILE
    slot_token = jnp.zeros((n_slots,), jnp.int32).at[slot].set(tok)
    tile_ids = jnp.minimum(jnp.arange(nt, dtype=jnp.int32), n_tiles[0] - 1)
    tile_expert = jnp.minimum(jnp.sum((tile_end[None, :] <= tile_ids[:, None]).astype(jnp.int32), axis=1),
                              N_EXPERTS - 1)
    return (tile_expert, n_tiles.astype(jnp.int32), slot_token.reshape(nt, MOE_TILE),
            slot[:n_tokens], slot[n_tokens:])


def _tile_slots(slot_a, slot_b, tm):
    n = slot_a.shape[0] // tm
    return jnp.concatenate([slot_a.reshape(n, tm), slot_b.reshape(n, tm)], axis=1)


def kernel(x_prompt, x_sample, cache_k, cache_v, state_ret, page_table, w_in, sb_bias, w_branch_a, w_branch_b,
           w_out, gn_g, ln1_g, ln1_b, ln2_g, ln2_b, w_group_router, b_group_router, w_expert_router,
           b_expert_router, w1, w3, w2):
    depth = w_in.shape[0]
    assert depth == 1, "single-layer step"
    batch, seq, d = x_prompt.shape
    db, tn, _ = x_sample.shape
    sb_w = SB_HEADS * SB_HEAD_DIM
    qk_w = RET_HEADS * RET_DK
    v_w = RET_HEADS * RET_DV
    widths = (sb_w, sb_w, sb_w, qk_w, qk_w, v_w, v_w, d, d)
    offs = np.concatenate([[0], np.cumsum(widths)])
    o_q, o_k, o_rq, o_rv, o_ga, o_end = offs[0], offs[1], offs[3], offs[5], offs[7], offs[9]
    alpha = float((2.0 * depth) ** 0.25)
    sb_scale = float(SB_HEAD_DIM ** -0.5) * LOG2_E
    n_pages = page_table.shape[1]
    page = cache_k.shape[2]
    past = n_pages * page

    w_in0 = w_in[0]
    wa, wb, wo = w_branch_a[0], w_branch_b[0], w_out[0]
    bias = sb_bias[0] * LOG2_E
    gn = gn_g[0]
    w_route = jnp.concatenate(
        [w_group_router[0], w_expert_router[0].reshape(d, N_EXPERTS),
         jnp.zeros((d, ROUTE_LANES - N_GROUPS - N_EXPERTS), F32)], axis=1)
    b_route = jnp.concatenate(
        [b_group_router[0], b_expert_router[0].reshape(N_EXPERTS),
         jnp.zeros((ROUTE_LANES - N_GROUPS - N_EXPERTS,), F32)]).reshape(1, ROUTE_LANES)

    t_p = batch * seq
    t_s = db * tn
    t_all = t_p + t_s

    xp = x_prompt.reshape(t_p, d)
    xp16 = xp.astype(BF16)
    (q16,) = _mm(xp16, w_in0, o_q, sb_w, (BF16,), scale=sb_scale)
    k32, k16 = _mm(xp16, w_in0, o_k, sb_w, (F32, BF16))
    v32, v16 = _mm(xp16, w_in0, o_k + sb_w, sb_w, (F32, BF16))
    (rqk,) = _mm(xp16, w_in0, o_rq, 2 * qk_w, (F32,))
    (rvg,) = _mm(xp16, w_in0, o_rv, 2 * v_w, (BF16,))
    (gab,) = _mm(xp16, w_in0, o_ga, 2 * d, (BF16,))

    o_sb = _sb_prompt(q16, k16, v16, bias, batch, seq)

    n_chunks = seq // RET_CHUNK
    tables_p = _ret_tables(np.arange(seq), RET_CHUNK, RET_CHUNK)
    s0_p = jnp.zeros((batch, RET_HEADS, RET_DK, RET_DV), F32)
    ret_p, st_p = _retention((rqk, 0), (rqk, 1), (rvg, 0), (rvg, 1), gn, s0_p, batch, n_chunks, tables_p,
                             hp=False, out_dtype=BF16)
    m_p = _merge(o_sb, ret_p, wa, wb, gab, hp=False, out_dtype=BF16)
    tm_h = 256
    assert t_s <= tm_h
    h_p, h3_all, route_p = _out_ln(m_p, wo, xp, ln1_g[0], ln1_b[0], w_route, b_route,
                                   hp=False, alpha=alpha, spare_tiles=1, tm=tm_h)

    xs = x_sample.reshape(t_s, d)
    (p_s,) = _mm(xs, w_in0, 0, int(o_end), (F32,), hp=True, tn=512)
    k_s = p_s[:, o_k:o_k + sb_w]
    v_s = p_s[:, o_k + sb_w:o_rq]
    q_s = p_s[:, o_q:o_k] * sb_scale
    o_sb_s = _sb_sample(q_s.reshape(db, tn, sb_w), k_s.reshape(db, tn, sb_w), v_s.reshape(db, tn, sb_w),
                        cache_k[0], cache_v[0], page_table, bias)

    def pad_rows(a):
        return jnp.pad(a.reshape(db, tn, -1), ((0, 0), (0, RET_CHUNK - tn), (0, 0))).reshape(db * RET_CHUNK, -1)

    pos_s = past + np.arange(RET_CHUNK)
    tables_s = _ret_tables(pos_s, tn, RET_CHUNK)
    rqk_s = pad_rows(p_s[:, o_rq:o_rv])
    rvg_s = pad_rows(p_s[:, o_rv:o_ga])
    ret_s_pad, st_s = _retention((rqk_s, 0), (rqk_s, 1), (rvg_s, 0), (rvg_s, 1), gn, state_ret[0], db, 1,
                                 tables_s, hp=True, out_dtype=F32)
    ret_s = ret_s_pad.reshape(db, RET_CHUNK, v_w)[:, :tn].reshape(t_s, v_w)
    m_s = _merge(o_sb_s.reshape(t_s, sb_w), ret_s, wa, wb, p_s[:, o_ga:o_end], hp=True, out_dtype=F32)
    h_s, h3_s, route_s = _out_ln(m_s, wo, xs, ln1_g[0], ln1_b[0], w_route, b_route, hp=True, alpha=alpha)

    h3_all = lax.dynamic_update_slice(h3_all, h3_s, (t_p, 0, 0))
    route = jnp.concatenate([route_p, route_s], axis=0)
    tile_expert, n_tiles, slot_token, slot_a, slot_b = _moe_plan(route, t_all)
    y_sorted = _moe_ffn(h3_all, tile_expert, n_tiles, slot_token, w1[0], w3[0], w2[0])
    tm_p = 256
    y_p = _combine_ln(h_p, t_p, _tile_slots(slot_a[:t_p], slot_b[:t_p], tm_p), route_p, y_sorted,
                      ln2_g[0], ln2_b[0], alpha=alpha, tm=tm_p)
    y_s = _combine_ln(h_s, t_s, _tile_slots(slot_a[t_p:], slot_b[t_p:], t_s), route_s, y_sorted,
                      ln2_g[0], ln2_b[0], alpha=alpha, tm=t_s)

    hs = (SB_HEADS, SB_HEAD_DIM)
    return (y_p.reshape(batch, seq, d), y_s.reshape(db, tn, d),
            k32.reshape(1, batch, seq, *hs), v32.reshape(1, batch, seq, *hs),
            st_p[None],
            k_s.reshape(1, db, tn, *hs), v_s.reshape(1, db, tn, *hs),
            st_s[None])
```

```python
import functools

import numpy as np
import jax
import jax.numpy as jnp
from jax import lax
from jax.experimental import pallas as pl
from jax.experimental.pallas import tpu as pltpu

F32 = jnp.float32
BF16 = jnp.bfloat16

SB_HEADS = 16
SB_HEAD_DIM = 128
RET_HEADS = 8
RET_DK = 128
RET_DV = 256
RET_CHUNK = 128
ROPE_BASE = 10000.0
N_GROUPS = 4
EXPERTS_PER_GROUP = 8
N_EXPERTS = N_GROUPS * EXPERTS_PER_GROUP
LN_EPS = 1e-5
GN_EPS = 1e-5

V7X_VMEM_LIMIT_BYTES = 56 * 1024 * 1024
LANES = 128
MOE_TILE = 256
ROUTE_LANES = LANES


def _cparams(sem):
    return pltpu.CompilerParams(dimension_semantics=sem, vmem_limit_bytes=V7X_VMEM_LIMIT_BYTES)


def _split_hi_lo(a):
    hi = a.astype(BF16)
    lo = (a - hi.astype(F32)).astype(BF16)
    return hi, lo


def _dot(a, b, hp, dims=(((1,), (0,)), ((), ()))):
    if not hp:
        return lax.dot_general(a.astype(BF16), b.astype(BF16), dims, preferred_element_type=F32)
    ah, al = _split_hi_lo(a.astype(F32))
    bh, bl = _split_hi_lo(b.astype(F32))
    d = functools.partial(lax.dot_general, dimension_numbers=dims, preferred_element_type=F32)
    return d(ah, bh) + (d(ah, bl) + d(al, bh))


_NT = (((1,), (1,)), ((), ()))
_TN = (((0,), (0,)), ((), ()))


def _mm_kernel(x_ref, w_ref, *refs, hp, scale):
    if hp:
        out_refs = refs
        acc = _dot(x_ref[...], w_ref[...], True)
    else:
        *out_refs, w16_ref = refs

        @pl.when(pl.program_id(1) == 0)
        def _():
            w16_ref[...] = w_ref[...].astype(BF16)

        acc = jnp.dot(x_ref[...], w16_ref[...], preferred_element_type=F32)
    if scale is not None:
        acc = acc * scale
    for o in out_refs:
        o[...] = acc.astype(o.dtype)


def _mm(x, w, col0, n, out_dtypes, *, hp=False, scale=None, tm=1024, tn=1024):
    m, k = x.shape
    col0, n = int(col0), int(n)
    tm = min(tm, m)
    tn = min(tn, n)
    assert m % tm == 0 and n % tn == 0 and col0 % tn == 0
    j0 = col0 // tn
    outs = pl.pallas_call(
        functools.partial(_mm_kernel, hp=hp, scale=scale),
        grid=(n // tn, m // tm),
        in_specs=[pl.BlockSpec((tm, k), lambda j, i: (i, 0)),
                  pl.BlockSpec((k, tn), lambda j, i: (0, j0 + j))],
        out_specs=[pl.BlockSpec((tm, tn), lambda j, i: (i, j)) for _ in out_dtypes],
        out_shape=[jax.ShapeDtypeStruct((m, n), dt) for dt in out_dtypes],
        scratch_shapes=[] if hp else [pltpu.VMEM((k, tn), BF16)],
        compiler_params=_cparams(("parallel", "arbitrary")),
        name="proj_hp" if hp else "proj",
    )(x, w)
    return outs


LOG2_E = 1.4426950408889634


EXP2_MAX_ARG = 126.0


def _softplus2(z2):
    return jnp.maximum(jnp.log2(1.0 + jnp.exp2(jnp.minimum(z2, EXP2_MAX_ARG))), z2)


def _strict_lower_ones(n):
    r = lax.broadcasted_iota(jnp.int32, (n, n), 0)
    c = lax.broadcasted_iota(jnp.int32, (n, n), 1)
    return jnp.where(r > c, 1.0, 0.0).astype(BF16)


SB_SUB = 256


def _sb_logit_stage(z2, valid, sub):
    s = _softplus2(z2)
    zs = z2 - s
    if valid is not None:
        s = jnp.where(valid, s, 0.0)
        zs = jnp.where(valid, zs, -jnp.inf)
    width = z2.shape[1]
    sub = min(sub, width)
    tri = _strict_lower_ones(sub)
    tails = []
    run = None
    for c in reversed(range(width // sub)):
        sc = s[:, c * sub:(c + 1) * sub]
        t = jnp.dot(sc.astype(BF16), tri, preferred_element_type=F32)
        rs = jnp.sum(sc, axis=1, keepdims=True)
        if run is None:
            run = rs
        else:
            t = t + run
            run = run + rs
        tails.append(t)
    return zs, jnp.concatenate(tails[::-1], axis=1), run


def _sb_weights(zs, tail):
    return jnp.exp2(zs - tail).astype(BF16)


def _sb_prompt_kernel(bias_ref, q_ref, k_ref, v_ref, o_ref, acc_ref, zs_ref, tail_ref, carry_ref,
                      rowsum_ref, *, tq):
    h = pl.program_id(1)
    qi = pl.program_id(2)
    lane = lax.broadcasted_iota(jnp.int32, (tq, SB_HEAD_DIM), 1)
    b = jnp.full((tq, SB_HEAD_DIM), bias_ref[h], F32)
    p0 = b.astype(BF16).astype(F32)
    p1 = (b - p0).astype(BF16).astype(F32)
    p2 = ((b - p0) - p1).astype(BF16).astype(F32)
    q_bias = jnp.where(lane == 0, p0, jnp.where(lane == 1, p1, jnp.where(lane == 2, p2, 0.0))).astype(BF16)
    k_ones = jnp.where(lane < 3, 1.0, 0.0).astype(BF16)
    q = jnp.concatenate([q_ref[...], q_bias], axis=1)
    acc_ref[...] = jnp.zeros_like(acc_ref)
    carry_ref[...] = jnp.zeros_like(carry_ref)

    def rows(j):
        return pl.ds(pl.multiple_of(j * tq, tq), tq)

    def logits(j, masked, p):
        k = jnp.concatenate([k_ref[rows(j), :], k_ones], axis=1)
        z2 = lax.dot_general(q, k, _NT, preferred_element_type=F32)
        valid = None
        if masked:
            valid = (lax.broadcasted_iota(jnp.int32, (tq, tq), 1)
                     < lax.broadcasted_iota(jnp.int32, (tq, tq), 0))
        zs_ref[p], tail_ref[p], rowsum_ref[p] = _sb_logit_stage(z2, valid, SB_SUB)

    def weights(j, p):
        w = _sb_weights(zs_ref[p], tail_ref[p])
        carry = carry_ref[...]
        acc_ref[...] += jnp.dot(w, v_ref[rows(j), :], preferred_element_type=F32) * jnp.exp2(-carry)
        carry_ref[...] = carry + rowsum_ref[p]

    def step(m, p):
        weights(qi - m, p)
        logits(qi - m - 1, False, 1 - p)

    logits(qi, True, 0)

    def pair(t, c):
        step(2 * t, 0)
        step(2 * t + 1, 1)
        return c

    lax.fori_loop(0, qi // 2, pair, 0)

    @pl.when(qi % 2 == 1)
    def _():
        step(qi - 1, 0)

    weights(0, qi % 2)
    o_ref[...] = acc_ref[...].astype(o_ref.dtype)


def _sb_prompt(q, k, v, sb_bias, batch, seq, *, tq=1024):
    tq = min(tq, seq)
    nq = seq // tq
    d = SB_HEAD_DIM
    return pl.pallas_call(
        functools.partial(_sb_prompt_kernel, tq=tq),
        grid=(batch, SB_HEADS, nq),
        in_specs=[pl.BlockSpec(memory_space=pltpu.SMEM),
                  pl.BlockSpec((tq, d), lambda b, h, i: (b * nq + i, h)),
                  pl.BlockSpec((seq, d), lambda b, h, i: (b, h)),
                  pl.BlockSpec((seq, d), lambda b, h, i: (b, h))],
        out_specs=pl.BlockSpec((tq, d), lambda b, h, i: (b * nq + i, h)),
        out_shape=jax.ShapeDtypeStruct(q.shape, BF16),
        scratch_shapes=[pltpu.VMEM((tq, d), F32), pltpu.VMEM((2, tq, tq), F32), pltpu.VMEM((2, tq, tq), F32),
                        pltpu.VMEM((tq, 1), F32), pltpu.VMEM((2, tq, 1), F32)],
        compiler_params=_cparams(("parallel", "parallel", "arbitrary")),
        name="sb_prompt",
    )(sb_bias.astype(F32), q, k, v)


SB_SAMPLE_PAGES = 4


def _sb_sample_kernel(pt_ref, qbd_ref, brow_ref, knew_ref, vnew_ref, *refs, n_groups, dec_seq):
    npg = SB_SAMPLE_PAGES
    k_refs, v_refs = refs[:npg], refs[npg:2 * npg]
    o_ref, acc_ref, carry_ref = refs[2 * npg:]
    g = pl.program_id(1)
    qbd = qbd_ref[...]
    n_rows = qbd.shape[0]
    page = knew_ref.shape[0]

    def attend(k_all, v_all, valid):
        z2 = lax.dot_general(qbd, k_all, _NT, preferred_element_type=F32) + brow_ref[...]
        zs, tail, rowsum = _sb_logit_stage(z2, valid, SB_SUB)
        w = _sb_weights(zs, tail)
        carry = carry_ref[...]
        acc_ref[...] += jnp.dot(w, v_all, preferred_element_type=F32) * jnp.exp2(-carry)
        carry_ref[...] = carry + rowsum

    @pl.when(g == 0)
    def _():
        acc_ref[...] = jnp.zeros_like(acc_ref)
        carry_ref[...] = jnp.zeros_like(carry_ref)
        t = lax.broadcasted_iota(jnp.int32, (n_rows, page), 0) % dec_seq
        s = lax.broadcasted_iota(jnp.int32, (n_rows, page), 1)
        attend(knew_ref[...], vnew_ref[...], s < t)

    @pl.when(g > 0)
    def _():
        def rows_of(ref):
            x = jnp.swapaxes(ref[...].astype(BF16), 0, 1)
            return jnp.concatenate([x[h] for h in range(SB_HEADS)], axis=1)

        attend(jnp.concatenate([rows_of(r) for r in k_refs], axis=0),
               jnp.concatenate([rows_of(r) for r in v_refs], axis=0), None)

    @pl.when(g == n_groups)
    def _():
        d = SB_HEAD_DIM
        for h in range(SB_HEADS):
            o_ref[:, h * d:(h + 1) * d] = acc_ref[h * dec_seq:(h + 1) * dec_seq, h * d:(h + 1) * d]


def _sb_sample(q_s, k_new, v_new, cache_k, cache_v, page_table, sb_bias):
    db, tn, w = q_s.shape
    n_pages = page_table.shape[1]
    n_pool, page = cache_k.shape[:2]
    nh, d = SB_HEADS, SB_HEAD_DIM
    npg = SB_SAMPLE_PAGES
    assert n_pages % npg == 0 and tn <= page
    n_groups = n_pages // npg
    n_rows = nh * tn
    qh = q_s.reshape(db, tn, nh, d).transpose(0, 2, 1, 3)
    eye = jnp.eye(nh, dtype=F32)
    qbd = (qh[:, :, :, None, :] * eye[None, :, None, :, None]).reshape(db, n_rows, w).astype(BF16)
    brow = jnp.repeat(sb_bias.astype(F32), tn).reshape(n_rows, 1)
    pad = ((0, 0), (0, page - tn), (0, 0))
    knew = jnp.pad(k_new, pad).astype(BF16)
    vnew = jnp.pad(v_new, pad).astype(BF16)

    def cache_map(p):
        return lambda b, g, pt: (pt[b * n_pages + n_pages - jnp.maximum(g, 1) * npg + p], 0, 0, 0)

    per_b = lambda b, g, pt: (b, 0, 0)
    const = lambda b, g, pt: (0, 0)
    grid_spec = pltpu.PrefetchScalarGridSpec(
        num_scalar_prefetch=1,
        grid=(db, n_groups + 1),
        in_specs=[pl.BlockSpec((None, n_rows, w), per_b),
                  pl.BlockSpec((n_rows, 1), const),
                  pl.BlockSpec((None, page, w), per_b),
                  pl.BlockSpec((None, page, w), per_b)]
        + [pl.BlockSpec((None, page, nh, d), cache_map(p)) for p in range(npg)] * 2,
        out_specs=pl.BlockSpec((None, tn, w), per_b),
        scratch_shapes=[pltpu.VMEM((n_rows, w), F32), pltpu.VMEM((n_rows, 1), F32)],
    )
    return pl.pallas_call(
        functools.partial(_sb_sample_kernel, n_groups=n_groups, dec_seq=tn),
        grid_spec=grid_spec,
        out_shape=jax.ShapeDtypeStruct((db, tn, w), F32),
        compiler_params=_cparams(("parallel", "arbitrary")),
        name="sb_sample",
    )(page_table.reshape(-1).astype(jnp.int32), qbd, brow, knew, vnew,
      *([cache_k] * npg), *([cache_v] * npg))


def _ret_tables(positions, chunk_len, tile):
    half = RET_DK // 2
    inv = np.float32(ROPE_BASE) ** (-(np.arange(half, dtype=np.float32) / np.float32(half)))
    ang = positions.astype(np.float32)[:, None] * inv.astype(np.float32)[None, :]
    c = np.cos(ang.astype(np.float64)).astype(np.float32)
    s = np.sin(ang.astype(np.float64)).astype(np.float32)
    cos2 = np.concatenate([c, c], axis=1)
    sin2 = np.concatenate([-s, s], axis=1)
    hh = np.arange(RET_HEADS, dtype=np.float32)
    log_g = np.log(np.float32(1.0) - np.float32(2.0) ** (np.float32(-5.0) - hh)).astype(np.float32)
    i = np.arange(tile, dtype=np.float32)
    live = (np.arange(tile) < chunk_len)
    diff = i[:, None] - i[None, :]
    decay = np.where((diff >= 0)[None] & live[None, :, None] & live[None, None, :],
                     np.exp(np.maximum(diff, 0.0)[None] * log_g[:, None, None]), 0.0).astype(np.float32)
    q_dec = np.where(live[None], np.exp((i + 1.0)[None] * log_g[:, None]), 0.0).astype(np.float32)
    k_dec = np.where(live[None], np.exp((np.float32(chunk_len) - 1.0 - i)[None] * log_g[:, None]),
                     0.0).astype(np.float32)
    c_dec = np.exp(np.float32(chunk_len) * log_g).astype(np.float32)
    k_scale = np.float32(RET_DK) ** np.float32(-0.5)
    qdec_full = np.repeat(q_dec.T, RET_DK, axis=1)
    kdec_full = np.repeat(k_dec.T, RET_DK, axis=1)
    cdec_full = np.broadcast_to(c_dec[:, None, None], (RET_HEADS, 1, RET_DV)).copy()
    return (jnp.asarray(cos2), jnp.asarray(sin2), jnp.asarray(decay), jnp.asarray(qdec_full),
            jnp.asarray(kdec_full), jnp.asarray(cdec_full), float(k_scale))


def _retention_kernel(rq_ref, rk_ref, rv_ref, rg_ref, cos_ref, sin_ref, decay_ref, qdec_ref, kdec_ref,
                      cdec_ref, gn_ref, s0_ref, ret_ref, sfin_ref, state_ref, *, hp, k_scale):
    c = pl.program_id(1)

    @pl.when(c == 0)
    def _():
        state_ref[...] = s0_ref[...]

    cos = cos_ref[...]
    sin = sin_ref[...]

    def rope(x):
        return x * cos + pltpu.roll(x, RET_DK // 2, 1) * sin

    for h in range(RET_HEADS):
        ks = slice(h * RET_DK, (h + 1) * RET_DK)
        vs = slice(h * RET_DV, (h + 1) * RET_DV)
        q = rope(rq_ref[:, ks].astype(F32))
        k = rope(rk_ref[:, ks].astype(F32)) * k_scale
        v = rv_ref[:, vs]
        scores = _dot(q, k, hp, _NT) * decay_ref[h]
        intra = _dot(scores, v, hp)
        state = state_ref[h]
        inter = _dot(q * qdec_ref[:, ks], state, hp)
        state_ref[h] = state * cdec_ref[h] + _dot(k * kdec_ref[:, ks], v, hp, _TN)
        o = intra + inter
        mu = jnp.mean(o, axis=-1, keepdims=True)
        cen = o - mu
        var = jnp.mean(cen * cen, axis=-1, keepdims=True)
        g = rg_ref[:, vs].astype(F32)
        ret = cen * lax.rsqrt(var + GN_EPS) * gn_ref[:, vs] * (g * jax.nn.sigmoid(g))
        ret_ref[:, vs] = ret.astype(ret_ref.dtype)

    @pl.when(c == pl.num_programs(1) - 1)
    def _():
        sfin_ref[...] = state_ref[...]


def _retention(rq, rk, rv, rg, gn_g, s0, batch, n_chunks, tables, *, hp, out_dtype):
    cos2, sin2, decay, qdec, kdec, cdec, k_scale = tables
    tile = decay.shape[1]
    wk = RET_HEADS * RET_DK
    wv = RET_HEADS * RET_DV
    (rq, cq), (rk, ck), (rv, cv), (rg, cg) = rq, rk, rv, rg
    row = lambda b, c: (b * n_chunks + c, 0)
    rowc = lambda cb: (lambda b, c: (b * n_chunks + c, cb))
    const2 = lambda b, c: (0, 0)
    const3 = lambda b, c: (0, 0, 0)
    return pl.pallas_call(
        functools.partial(_retention_kernel, hp=hp, k_scale=k_scale),
        grid=(batch, n_chunks),
        in_specs=[pl.BlockSpec((tile, wk), rowc(cq)), pl.BlockSpec((tile, wk), rowc(ck)),
                  pl.BlockSpec((tile, wv), rowc(cv)), pl.BlockSpec((tile, wv), rowc(cg)),
                  pl.BlockSpec((tile, RET_DK), lambda b, c: (c, 0)),
                  pl.BlockSpec((tile, RET_DK), lambda b, c: (c, 0)),
                  pl.BlockSpec(decay.shape, const3),
                  pl.BlockSpec(qdec.shape, const2), pl.BlockSpec(kdec.shape, const2),
                  pl.BlockSpec(cdec.shape, const3),
                  pl.BlockSpec((1, wv), const2),
                  pl.BlockSpec((None, RET_HEADS, RET_DK, RET_DV), lambda b, c: (b, 0, 0, 0))],
        out_specs=[pl.BlockSpec((tile, wv), row),
                   pl.BlockSpec((None, RET_HEADS, RET_DK, RET_DV), lambda b, c: (b, 0, 0, 0))],
        out_shape=[jax.ShapeDtypeStruct((batch * n_chunks * tile, wv), out_dtype),
                   jax.ShapeDtypeStruct((batch, RET_HEADS, RET_DK, RET_DV), F32)],
        scratch_shapes=[pltpu.VMEM((RET_HEADS, RET_DK, RET_DV), F32)],
        compiler_params=_cparams(("parallel", "arbitrary")),
        name="retention_hp" if hp else "retention",
    )(rq, rk, rv, rg, cos2, sin2, decay, qdec, kdec, cdec, gn_g.reshape(1, wv).astype(F32), s0)


def _merge_kernel(osb_ref, ret_ref, wa_ref, wb_ref, ga_ref, gb_ref, m_ref, *scratch, hp):
    if hp:
        pa = _dot(osb_ref[...], wa_ref[...], True)
        pb = _dot(ret_ref[...], wb_ref[...], True)
    else:
        wa16_ref, wb16_ref = scratch

        @pl.when(pl.program_id(1) == 0)
        def _():
            wa16_ref[...] = wa_ref[...].astype(BF16)
            wb16_ref[...] = wb_ref[...].astype(BF16)

        pa = jnp.dot(osb_ref[...], wa16_ref[...], preferred_element_type=F32)
        pb = jnp.dot(ret_ref[...], wb16_ref[...], preferred_element_type=F32)
    m = jax.nn.sigmoid(ga_ref[...].astype(F32)) * pa + jax.nn.sigmoid(gb_ref[...].astype(F32)) * pb
    m_ref[...] = m.astype(m_ref.dtype)


def _merge(osb, ret, wa, wb, gab, *, hp, out_dtype, tm=1024, tn=512):
    m, k = osb.shape
    n = wa.shape[1]
    tm = min(tm, m)
    nb = n // tn
    row = pl.BlockSpec((tm, k), lambda j, i: (i, 0))
    wsp = pl.BlockSpec((k, tn), lambda j, i: (0, j))
    tile = pl.BlockSpec((tm, tn), lambda j, i: (i, j))
    tile_b = pl.BlockSpec((tm, tn), lambda j, i: (i, nb + j))
    return pl.pallas_call(
        functools.partial(_merge_kernel, hp=hp),
        grid=(nb, m // tm),
        in_specs=[row, row, wsp, wsp, tile, tile_b],
        out_specs=tile,
        out_shape=jax.ShapeDtypeStruct((m, n), out_dtype),
        scratch_shapes=[] if hp else [pltpu.VMEM((k, tn), BF16), pltpu.VMEM((k, tn), BF16)],
        compiler_params=_cparams(("parallel", "arbitrary")),
        name="merge_hp" if hp else "merge",
    )(osb, ret, wa, wb, gab, gab)


def _layer_norm(x, g, b):
    mu = jnp.mean(x, axis=-1, keepdims=True)
    cen = x - mu
    var = jnp.mean(cen * cen, axis=-1, keepdims=True)
    return cen * lax.rsqrt(var + LN_EPS) * g + b


def _route(logits):
    lane = lax.broadcasted_iota(jnp.int32, logits.shape, 1)
    big = jnp.int32(ROUTE_LANES)
    neg = -jnp.inf

    def first_max(x):
        m = jnp.max(x, axis=1, keepdims=True)
        idx = jnp.min(jnp.where(x == m, lane, big), axis=1, keepdims=True)
        return m, idx

    gl = jnp.where(lane < N_GROUPS, logits, neg)
    gmax, gidx = first_max(gl)
    g_weight = 1.0 / jnp.sum(jnp.exp(gl - gmax), axis=1, keepdims=True)
    lo = N_GROUPS + gidx * EXPERTS_PER_GROUP
    el = jnp.where((lane >= lo) & (lane < lo + EXPERTS_PER_GROUP), logits, neg)
    v0, i0 = first_max(el)
    v1, i1 = first_max(jnp.where(lane == i0, neg, el))
    e = jnp.exp(v1 - v0)
    w0 = 1.0 / (1.0 + e)
    w1 = e / (1.0 + e)
    vals = [(i0 - N_GROUPS).astype(F32), (i1 - N_GROUPS).astype(F32), g_weight * w0, g_weight * w1]
    out = jnp.zeros(logits.shape, F32)
    for n, val in enumerate(vals):
        out = jnp.where(lane == n, val, out)
    return out


def _rows_to_tiles(x):
    t, w = x.shape
    return x.reshape(t, w // LANES, LANES)


def _tiles_to_rows(x3):
    x = jnp.swapaxes(x3, 0, 1)
    return jnp.concatenate([x[c] for c in range(x.shape[0])], axis=1)


def _out_ln_kernel(m_ref, wo_ref, x_ref, g_ref, b_ref, wr_ref, br_ref, h_ref, h3_ref, route_ref,
                   *scratch, hp, alpha, n_real):
    i = pl.program_id(0)
    if not hp:
        (wo16_ref,) = scratch

        @pl.when(i == 0)
        def _():
            wo16_ref[...] = wo_ref[...].astype(BF16)

    @pl.when(i < n_real)
    def _():
        if hp:
            mix = _dot(m_ref[...], wo_ref[...], True)
        else:
            mix = jnp.dot(m_ref[...], wo16_ref[...], preferred_element_type=F32)
        h = _layer_norm(alpha * x_ref[...] + mix, g_ref[...], b_ref[...])
        h_ref[...] = h
        h3_ref[...] = _rows_to_tiles(h.astype(BF16))
        logits = _dot(h, wr_ref[...], True) + br_ref[...]
        route_ref[...] = _route(logits)

    @pl.when(i >= n_real)
    def _():
        h3_ref[...] = jnp.zeros_like(h3_ref)


def _out_ln(m, w_out, x, ln_g, ln_b, w_route, b_route, *, hp, alpha, spare_tiles=0, tm=256):
    t, d = x.shape
    tm = min(tm, t)
    n_real = t // tm
    const = lambda i: (0, 0)
    row = lambda i: (jnp.minimum(i, n_real - 1), 0)
    return pl.pallas_call(
        functools.partial(_out_ln_kernel, hp=hp, alpha=alpha, n_real=n_real),
        grid=(n_real + spare_tiles,),
        in_specs=[pl.BlockSpec((tm, d), row),
                  pl.BlockSpec((d, d), const, pipeline_mode=pl.Buffered(1)),
                  pl.BlockSpec((tm, d), row),
                  pl.BlockSpec((1, d), const), pl.BlockSpec((1, d), const),
                  pl.BlockSpec((d, ROUTE_LANES), const), pl.BlockSpec((1, ROUTE_LANES), const)],
        out_specs=[pl.BlockSpec((tm, d), row),
                   pl.BlockSpec((tm, d // LANES, LANES), lambda i: (i, 0, 0)),
                   pl.BlockSpec((tm, ROUTE_LANES), row)],
        out_shape=[jax.ShapeDtypeStruct((t, d), F32),
                   jax.ShapeDtypeStruct((t + spare_tiles * tm, d // LANES, LANES), BF16),
                   jax.ShapeDtypeStruct((t, ROUTE_LANES), F32)],
        scratch_shapes=[] if hp else [pltpu.VMEM((d, d), BF16)],
        compiler_params=_cparams(("arbitrary",)),
        name="out_ln_hp" if hp else "out_ln",
    )(m, w_out, x, ln_g.reshape(1, d), ln_b.reshape(1, d), w_route, b_route)


def _row_copy(src_hbm, row, dst, dst_row, sem):
    return pltpu.make_async_copy(src_hbm.at[pl.ds(row, 1)], dst.at[pl.ds(dst_row, 1)], sem)


def _gather_rows(src_hbm, idx_ref, tile, n, dst, sem, wait):
    for r in range(n):
        cp = _row_copy(src_hbm, idx_ref[tile, r], dst, r, sem)
        if wait:
            cp.wait()
        else:
            cp.start(priority=r % 2)


def _moe_ffn_kernel(te_ref, nt_ref, idx_ref, h_hbm, w1_ref, w3_ref, w2_ref, y_ref, xbuf, sem):
    t = pl.program_id(0)
    n_tiles = nt_ref[0]
    slot = t % 2
    rows = xbuf.shape[1]
    nxt = jnp.minimum(t + 1, n_tiles - 1)

    @pl.when(t == 0)
    def _():
        _gather_rows(h_hbm, idx_ref, 0, rows, xbuf.at[0], sem.at[0], wait=False)

    @pl.when(t < n_tiles)
    def _():
        _gather_rows(h_hbm, idx_ref, nxt, rows, xbuf.at[1 - slot], sem.at[1 - slot], wait=False)
        _gather_rows(h_hbm, idx_ref, t, rows, xbuf.at[slot], sem.at[slot], wait=True)
        x = _tiles_to_rows(xbuf[slot])
        a = jnp.dot(x, w1_ref[...].astype(BF16), preferred_element_type=F32)
        c = jnp.dot(x, w3_ref[...].astype(BF16), preferred_element_type=F32)
        hid = (a * jax.nn.sigmoid(a)) * c
        y = jnp.dot(hid.astype(BF16), w2_ref[...].astype(BF16), preferred_element_type=F32)
        y_ref[...] = _rows_to_tiles(y.astype(BF16))

    @pl.when(t == n_tiles - 1)
    def _():
        _gather_rows(h_hbm, idx_ref, nxt, rows, xbuf.at[1 - slot], sem.at[1 - slot], wait=True)

    @pl.when(t >= n_tiles)
    def _():
        y_ref[...] = jnp.zeros_like(y_ref)


def _moe_ffn(h_all, tile_expert, n_tiles, slot_token, w1, w3, w2):
    nt = slot_token.shape[0]
    nc = h_all.shape[1]
    d = nc * LANES
    ff = w1.shape[2]
    grid_spec = pltpu.PrefetchScalarGridSpec(
        num_scalar_prefetch=3,
        grid=(nt,),
        in_specs=[
            pl.BlockSpec(memory_space=pl.ANY),
            pl.BlockSpec((None, d, ff), lambda t, te, nv, ix: (te[t], 0, 0)),
            pl.BlockSpec((None, d, ff), lambda t, te, nv, ix: (te[t], 0, 0)),
            pl.BlockSpec((None, ff, d), lambda t, te, nv, ix: (te[t], 0, 0)),
        ],
        out_specs=pl.BlockSpec((MOE_TILE, nc, LANES), lambda t, te, nv, ix: (t, 0, 0)),
        scratch_shapes=[pltpu.VMEM((2, MOE_TILE, nc, LANES), BF16), pltpu.SemaphoreType.DMA((2,))],
    )
    return pl.pallas_call(
        _moe_ffn_kernel,
        grid_spec=grid_spec,
        out_shape=jax.ShapeDtypeStruct((nt * MOE_TILE, nc, LANES), BF16),
        compiler_params=_cparams(("arbitrary",)),
        name="moe_ffn",
    )(tile_expert, n_tiles, slot_token, h_all, w1, w3, w2)


def _combine_ln_kernel(idx_ref, h_ref, route_ref, g_ref, b_ref, y_hbm, o_ref, ybuf, sem, *, alpha):
    t = pl.program_id(0)
    nt = pl.num_programs(0)
    slot = t % 2
    rows = ybuf.shape[1]
    nxt = jnp.minimum(t + 1, nt - 1)

    @pl.when(t == 0)
    def _():
        _gather_rows(y_hbm, idx_ref, 0, rows, ybuf.at[0], sem.at[0], wait=False)

    _gather_rows(y_hbm, idx_ref, t, rows, ybuf.at[slot], sem.at[slot], wait=True)
    tm = rows // 2
    ya = _tiles_to_rows(ybuf[slot, pl.ds(0, tm)]).astype(F32)
    yb = _tiles_to_rows(ybuf[slot, pl.ds(tm, tm)]).astype(F32)
    _gather_rows(y_hbm, idx_ref, nxt, rows, ybuf.at[1 - slot], sem.at[1 - slot], wait=False)
    route = route_ref[...]
    moe = route[:, 2:3] * ya + route[:, 3:4] * yb
    o_ref[...] = _layer_norm(alpha * h_ref[...] + moe, g_ref[...], b_ref[...])

    @pl.when(t == nt - 1)
    def _():
        _gather_rows(y_hbm, idx_ref, nxt, rows, ybuf.at[1 - slot], sem.at[1 - slot], wait=True)


def _combine_ln(h, n_rows, tok_slots, route, y_sorted, ln_g, ln_b, *, alpha, tm):
    d = h.shape[1]
    nc = y_sorted.shape[1]
    nt = n_rows // tm
    const = lambda t, ix: (0, 0)
    row = lambda t, ix: (t, 0)
    grid_spec = pltpu.PrefetchScalarGridSpec(
        num_scalar_prefetch=1,
        grid=(nt,),
        in_specs=[pl.BlockSpec((tm, d), row),
                  pl.BlockSpec((tm, ROUTE_LANES), row),
                  pl.BlockSpec((1, d), const), pl.BlockSpec((1, d), const),
                  pl.BlockSpec(memory_space=pl.ANY)],
        out_specs=pl.BlockSpec((tm, d), row),
        scratch_shapes=[pltpu.VMEM((2, 2 * tm, nc, LANES), BF16), pltpu.SemaphoreType.DMA((2,))],
    )
    return pl.pallas_call(
        functools.partial(_combine_ln_kernel, alpha=alpha),
        grid_spec=grid_spec,
        out_shape=jax.ShapeDtypeStruct((n_rows, d), F32),
        compiler_params=_cparams(("arbitrary",)),
        name="combine_ln",
    )(tok_slots, h, route, ln_g.reshape(1, d), ln_b.reshape(1, d), y_sorted)


def _moe_plan(route, n_tokens):
    e = jnp.concatenate([route[:, 0], route[:, 1]]).astype(jnp.int32)
    tok = jnp.concatenate([jnp.arange(n_tokens, dtype=jnp.int32)] * 2)
    n_assign = 2 * n_tokens
    nt = (n_assign + N_EXPERTS * (MOE_TILE - 1)) // MOE_TILE
    onehot = (e[:, None] == jnp.arange(N_EXPERTS, dtype=jnp.int32)[None, :]).astype(jnp.int32)
    csum = jnp.cumsum(onehot, axis=0)
    counts = csum[-1]
    rank = jnp.take_along_axis(csum, e[:, None], axis=1)[:, 0] - 1
    tiles_per_e = (counts + MOE_TILE - 1) // MOE_TILE
    tile_end = jnp.cumsum(tiles_per_e)
    tile_start = tile_end - tiles_per_e
    n_tiles = tile_end[-1:]
    slot = tile_start[e] * MOE_TILE + rank
    n_slots = nt * MOE_TILE
    slot_token = jnp.zeros((n_slots,), jnp.int32).at[slot].set(tok)
    tile_ids = jnp.minimum(jnp.arange(nt, dtype=jnp.int32), n_tiles[0] - 1)
    tile_expert = jnp.minimum(jnp.sum((tile_end[None, :] <= tile_ids[:, None]).astype(jnp.int32), axis=1),
                              N_EXPERTS - 1)
    return (tile_expert, n_tiles.astype(jnp.int32), slot_token.reshape(nt, MOE_TILE),
            slot[:n_tokens], slot[n_tokens:])


def _tile_slots(slot_a, slot_b, tm):
    n = slot_a.shape[0] // tm
    return jnp.concatenate([slot_a.reshape(n, tm), slot_b.reshape(n, tm)], axis=1)


def kernel(x_prompt, x_sample, cache_k, cache_v, state_ret, page_table, w_in, sb_bias, w_branch_a, w_branch_b,
           w_out, gn_g, ln1_g, ln1_b, ln2_g, ln2_b, w_group_router, b_group_router, w_expert_router,
           b_expert_router, w1, w3, w2):
    depth = w_in.shape[0]
    assert depth == 1, "single-layer step"
    batch, seq, d = x_prompt.shape
    db, tn, _ = x_sample.shape
    sb_w = SB_HEADS * SB_HEAD_DIM
    qk_w = RET_HEADS * RET_DK
    v_w = RET_HEADS * RET_DV
    widths = (sb_w, sb_w, sb_w, qk_w, qk_w, v_w, v_w, d, d)
    offs = np.concatenate([[0], np.cumsum(widths)])
    o_q, o_k, o_rq, o_rv, o_ga, o_end = offs[0], offs[1], offs[3], offs[5], offs[7], offs[9]
    alpha = float((2.0 * depth) ** 0.25)
    sb_scale = float(SB_HEAD_DIM ** -0.5) * LOG2_E
    n_pages = page_table.shape[1]
    page = cache_k.shape[2]
    past = n_pages * page

    w_in0 = w_in[0]
    wa, wb, wo = w_branch_a[0], w_branch_b[0], w_out[0]
    bias = sb_bias[0] * LOG2_E
    gn = gn_g[0]
    w_route = jnp.concatenate(
        [w_group_router[0], w_expert_router[0].reshape(d, N_EXPERTS),
         jnp.zeros((d, ROUTE_LANES - N_GROUPS - N_EXPERTS), F32)], axis=1)
    b_route = jnp.concatenate(
        [b_group_router[0], b_expert_router[0].reshape(N_EXPERTS),
         jnp.zeros((ROUTE_LANES - N_GROUPS - N_EXPERTS,), F32)]).reshape(1, ROUTE_LANES)

    t_p = batch * seq
    t_s = db * tn
    t_all = t_p + t_s

    xp = x_prompt.reshape(t_p, d)
    xp16 = xp.astype(BF16)
    (q16,) = _mm(xp16, w_in0, o_q, sb_w, (BF16,), scale=sb_scale)
    k32, k16 = _mm(xp16, w_in0, o_k, sb_w, (F32, BF16))
    v32, v16 = _mm(xp16, w_in0, o_k + sb_w, sb_w, (F32, BF16))
    (rqk,) = _mm(xp16, w_in0, o_rq, 2 * qk_w, (F32,))
    (rvg,) = _mm(xp16, w_in0, o_rv, 2 * v_w, (BF16,))
    (gab,) = _mm(xp16, w_in0, o_ga, 2 * d, (BF16,))

    o_sb = _sb_prompt(q16, k16, v16, bias, batch, seq)

    n_chunks = seq // RET_CHUNK
    tables_p = _ret_tables(np.arange(seq), RET_CHUNK, RET_CHUNK)
    s0_p = jnp.zeros((batch, RET_HEADS, RET_DK, RET_DV), F32)
    ret_p, st_p = _retention((rqk, 0), (rqk, 1), (rvg, 0), (rvg, 1), gn, s0_p, batch, n_chunks, tables_p,
                             hp=False, out_dtype=BF16)
    m_p = _merge(o_sb, ret_p, wa, wb, gab, hp=False, out_dtype=BF16)
    tm_h = 256
    assert t_s <= tm_h
    h_p, h3_all, route_p = _out_ln(m_p, wo, xp, ln1_g[0], ln1_b[0], w_route, b_route,
                                   hp=False, alpha=alpha, spare_tiles=1, tm=tm_h)

    xs = x_sample.reshape(t_s, d)
    (p_s,) = _mm(xs, w_in0, 0, int(o_end), (F32,), hp=True, tn=512)
    k_s = p_s[:, o_k:o_k + sb_w]
    v_s = p_s[:, o_k + sb_w:o_rq]
    q_s = p_s[:, o_q:o_k] * sb_scale
    o_sb_s = _sb_sample(q_s.reshape(db, tn, sb_w), k_s.reshape(db, tn, sb_w), v_s.reshape(db, tn, sb_w),
                        cache_k[0], cache_v[0], page_table, bias)

    def pad_rows(a):
        return jnp.pad(a.reshape(db, tn, -1), ((0, 0), (0, RET_CHUNK - tn), (0, 0))).reshape(db * RET_CHUNK, -1)

    pos_s = past + np.arange(RET_CHUNK)
    tables_s = _ret_tables(pos_s, tn, RET_CHUNK)
    rqk_s = pad_rows(p_s[:, o_rq:o_rv])
    rvg_s = pad_rows(p_s[:, o_rv:o_ga])
    ret_s_pad, st_s = _retention((rqk_s, 0), (rqk_s, 1), (rvg_s, 0), (rvg_s, 1), gn, state_ret[0], db, 1,
                                 tables_s, hp=True, out_dtype=F32)
    ret_s = ret_s_pad.reshape(db, RET_CHUNK, v_w)[:, :tn].reshape(t_s, v_w)
    m_s = _merge(o_sb_s.reshape(t_s, sb_w), ret_s, wa, wb, p_s[:, o_ga:o_end], hp=True, out_dtype=F32)
    h_s, h3_s, route_s = _out_ln(m_s, wo, xs, ln1_g[0], ln1_b[0], w_route, b_route, hp=True, alpha=alpha)

    h3_all = lax.dynamic_update_slice(h3_all, h3_s, (t_p, 0, 0))
    route = jnp.concatenate([route_p, route_s], axis=0)
    tile_expert, n_tiles, slot_token, slot_a, slot_b = _moe_plan(route, t_all)
    y_sorted = _moe_ffn(h3_all, tile_expert, n_tiles, slot_token, w1[0], w3[0], w2[0])
    tm_p = 256
    y_p = _combine_ln(h_p, t_p, _tile_slots(slot_a[:t_p], slot_b[:t_p], tm_p), route_p, y_sorted,
                      ln2_g[0], ln2_b[0], alpha=alpha, tm=tm_p)
    y_s = _combine_ln(h_s, t_s, _tile_slots(slot_a[t_p:], slot_b[t_p:], t_s), route_s, y_sorted,
                      ln2_g[0], ln2_b[0], alpha=alpha, tm=t_s)

    hs = (SB_HEADS, SB_HEAD_DIM)
    return (y_p.reshape(batch, seq, d), y_s.reshape(db, tn, d),
            k32.reshape(1, batch, seq, *hs), v32.reshape(1, batch, seq, *hs),
            st_p[None],
            k_s.reshape(1, db, tn, *hs), v_s.reshape(1, db, tn, *hs),
            st_s[None])
```

```python
import functools

import numpy as np
import jax
import jax.numpy as jnp
from jax import lax
from jax.experimental import pallas as pl
from jax.experimental.pallas import tpu as pltpu

F32 = jnp.float32
BF16 = jnp.bfloat16

SB_HEADS = 16
SB_HEAD_DIM = 128
RET_HEADS = 8
RET_DK = 128
RET_DV = 256
RET_CHUNK = 128
ROPE_BASE = 10000.0
N_GROUPS = 4
EXPERTS_PER_GROUP = 8
N_EXPERTS = N_GROUPS * EXPERTS_PER_GROUP
LN_EPS = 1e-5
GN_EPS = 1e-5

V7X_VMEM_LIMIT_BYTES = 56 * 1024 * 1024
LANES = 128
MOE_TILE = 256
ROUTE_LANES = LANES


def _cparams(sem):
    return pltpu.CompilerParams(dimension_semantics=sem, vmem_limit_bytes=V7X_VMEM_LIMIT_BYTES)


def _split_hi_lo(a):
    hi = a.astype(BF16)
    lo = (a - hi.astype(F32)).astype(BF16)
    return hi, lo


def _dot(a, b, hp, dims=(((1,), (0,)), ((), ()))):
    if not hp:
        return lax.dot_general(a.astype(BF16), b.astype(BF16), dims, preferred_element_type=F32)
    ah, al = _split_hi_lo(a.astype(F32))
    bh, bl = _split_hi_lo(b.astype(F32))
    d = functools.partial(lax.dot_general, dimension_numbers=dims, preferred_element_type=F32)
    return d(ah, bh) + (d(ah, bl) + d(al, bh))


_NT = (((1,), (1,)), ((), ()))
_TN = (((0,), (0,)), ((), ()))


def _mm_kernel(x_ref, w_ref, *refs, hp, scale):
    if hp:
        out_refs = refs
        acc = _dot(x_ref[...], w_ref[...], True)
    else:
        *out_refs, w16_ref = refs

        @pl.when(pl.program_id(1) == 0)
        def _():
            w16_ref[...] = w_ref[...].astype(BF16)

        acc = jnp.dot(x_ref[...], w16_ref[...], preferred_element_type=F32)
    if scale is not None:
        acc = acc * scale
    for o in out_refs:
        o[...] = acc.astype(o.dtype)


def _mm(x, w, col0, n, out_dtypes, *, hp=False, scale=None, tm=1024, tn=1024):
    m, k = x.shape
    col0, n = int(col0), int(n)
    tm = min(tm, m)
    tn = min(tn, n)
    assert m % tm == 0 and n % tn == 0 and col0 % tn == 0
    j0 = col0 // tn
    outs = pl.pallas_call(
        functools.partial(_mm_kernel, hp=hp, scale=scale),
        grid=(n // tn, m // tm),
        in_specs=[pl.BlockSpec((tm, k), lambda j, i: (i, 0)),
                  pl.BlockSpec((k, tn), lambda j, i: (0, j0 + j))],
        out_specs=[pl.BlockSpec((tm, tn), lambda j, i: (i, j)) for _ in out_dtypes],
        out_shape=[jax.ShapeDtypeStruct((m, n), dt) for dt in out_dtypes],
        scratch_shapes=[] if hp else [pltpu.VMEM((k, tn), BF16)],
        compiler_params=_cparams(("parallel", "arbitrary")),
        name="proj_hp" if hp else "proj",
    )(x, w)
    return outs


LOG2_E = 1.4426950408889634


EXP2_MAX_ARG = 126.0


def _softplus2(z2):
    return jnp.maximum(jnp.log2(1.0 + jnp.exp2(jnp.minimum(z2, EXP2_MAX_ARG))), z2)


def _strict_lower_ones(n):
    r = lax.broadcasted_iota(jnp.int32, (n, n), 0)
    c = lax.broadcasted_iota(jnp.int32, (n, n), 1)
    return jnp.where(r > c, 1.0, 0.0).astype(BF16)


SB_SUB = 256


def _sb_logit_stage(z2, valid, sub):
    s = _softplus2(z2)
    zs = z2 - s
    if valid is not None:
        s = jnp.where(valid, s, 0.0)
        zs = jnp.where(valid, zs, -jnp.inf)
    width = z2.shape[1]
    sub = min(sub, width)
    tri = _strict_lower_ones(sub)
    tails = []
    run = None
    for c in reversed(range(width // sub)):
        sc = s[:, c * sub:(c + 1) * sub]
        t = jnp.dot(sc.astype(BF16), tri, preferred_element_type=F32)
        rs = jnp.sum(sc, axis=1, keepdims=True)
        if run is None:
            run = rs
        else:
            t = t + run
            run = run + rs
        tails.append(t)
    return zs, jnp.concatenate(tails[::-1], axis=1), run


def _sb_weights(zs, tail):
    return jnp.exp2(zs - tail).astype(BF16)


def _sb_prompt_kernel(bias_ref, q_ref, k_ref, v_ref, o_ref, acc_ref, zs_ref, tail_ref, carry_ref,
                      rowsum_ref, *, tq):
    h = pl.program_id(1)
    qi = pl.program_id(2)
    lane = lax.broadcasted_iota(jnp.int32, (tq, SB_HEAD_DIM), 1)
    b = jnp.full((tq, SB_HEAD_DIM), bias_ref[h], F32)
    p0 = b.astype(BF16).astype(F32)
    p1 = (b - p0).astype(BF16).astype(F32)
    p2 = ((b - p0) - p1).astype(BF16).astype(F32)
    q_bias = jnp.where(lane == 0, p0, jnp.where(lane == 1, p1, jnp.where(lane == 2, p2, 0.0))).astype(BF16)
    k_ones = jnp.where(lane < 3, 1.0, 0.0).astype(BF16)
    q = jnp.concatenate([q_ref[...], q_bias], axis=1)
    acc_ref[...] = jnp.zeros_like(acc_ref)
    carry_ref[...] = jnp.zeros_like(carry_ref)

    def rows(j):
        return pl.ds(pl.multiple_of(j * tq, tq), tq)

    def logits(j, masked, p):
        k = jnp.concatenate([k_ref[rows(j), :], k_ones], axis=1)
        z2 = lax.dot_general(q, k, _NT, preferred_element_type=F32)
        valid = None
        if masked:
            valid = (lax.broadcasted_iota(jnp.int32, (tq, tq), 1)
                     < lax.broadcasted_iota(jnp.int32, (tq, tq), 0))
        zs_ref[p], tail_ref[p], rowsum_ref[p] = _sb_logit_stage(z2, valid, SB_SUB)

    def weights(j, p):
        w = _sb_weights(zs_ref[p], tail_ref[p])
        carry = carry_ref[...]
        acc_ref[...] += jnp.dot(w, v_ref[rows(j), :], preferred_element_type=F32) * jnp.exp2(-carry)
        carry_ref[...] = carry + rowsum_ref[p]

    def step(m, p):
        weights(qi - m, p)
        logits(qi - m - 1, False, 1 - p)

    logits(qi, True, 0)

    def pair(t, c):
        step(2 * t, 0)
        step(2 * t + 1, 1)
        return c

    lax.fori_loop(0, qi // 2, pair, 0)

    @pl.when(qi % 2 == 1)
    def _():
        step(qi - 1, 0)

    weights(0, qi % 2)
    o_ref[...] = acc_ref[...].astype(o_ref.dtype)


def _sb_prompt(q, k, v, sb_bias, batch, seq, *, tq=1024):
    tq = min(tq, seq)
    nq = seq // tq
    d = SB_HEAD_DIM
    return pl.pallas_call(
        functools.partial(_sb_prompt_kernel, tq=tq),
        grid=(batch, SB_HEADS, nq),
        in_specs=[pl.BlockSpec(memory_space=pltpu.SMEM),
                  pl.BlockSpec((tq, d), lambda b, h, i: (b * nq + i, h)),
                  pl.BlockSpec((seq, d), lambda b, h, i: (b, h)),
                  pl.BlockSpec((seq, d), lambda b, h, i: (b, h))],
        out_specs=pl.BlockSpec((tq, d), lambda b, h, i: (b * nq + i, h)),
        out_shape=jax.ShapeDtypeStruct(q.shape, BF16),
        scratch_shapes=[pltpu.VMEM((tq, d), F32), pltpu.VMEM((2, tq, tq), F32), pltpu.VMEM((2, tq, tq), F32),
                        pltpu.VMEM((tq, 1), F32), pltpu.VMEM((2, tq, 1), F32)],
        compiler_params=_cparams(("parallel", "parallel", "arbitrary")),
        name="sb_prompt",
    )(sb_bias.astype(F32), q, k, v)


SB_SAMPLE_PAGES = 4


def _sb_sample_kernel(pt_ref, qbd_ref, brow_ref, knew_ref, vnew_ref, *refs, n_groups, dec_seq):
    npg = SB_SAMPLE_PAGES
    k_refs, v_refs = refs[:npg], refs[npg:2 * npg]
    o_ref, acc_ref, carry_ref = refs[2 * npg:]
    g = pl.program_id(1)
    qbd = qbd_ref[...]
    n_rows = qbd.shape[0]
    page = knew_ref.shape[0]

    def attend(k_all, v_all, valid):
        z2 = lax.dot_general(qbd, k_all, _NT, preferred_element_type=F32) + brow_ref[...]
        zs, tail, rowsum = _sb_logit_stage(z2, valid, SB_SUB)
        w = _sb_weights(zs, tail)
        carry = carry_ref[...]
        acc_ref[...] += jnp.dot(w, v_all, preferred_element_type=F32) * jnp.exp2(-carry)
        carry_ref[...] = carry + rowsum

    @pl.when(g == 0)
    def _():
        acc_ref[...] = jnp.zeros_like(acc_ref)
        carry_ref[...] = jnp.zeros_like(carry_ref)
        t = lax.broadcasted_iota(jnp.int32, (n_rows, page), 0) % dec_seq
        s = lax.broadcasted_iota(jnp.int32, (n_rows, page), 1)
        attend(knew_ref[...], vnew_ref[...], s < t)

    @pl.when(g > 0)
    def _():
        def rows_of(ref):
            x = jnp.swapaxes(ref[...].astype(BF16), 0, 1)
            return jnp.concatenate([x[h] for h in range(SB_HEADS)], axis=1)

        attend(jnp.concatenate([rows_of(r) for r in k_refs], axis=0),
               jnp.concatenate([rows_of(r) for r in v_refs], axis=0), None)

    @pl.when(g == n_groups)
    def _():
        d = SB_HEAD_DIM
        for h in range(SB_HEADS):
            o_ref[:, h * d:(h + 1) * d] = acc_ref[h * dec_seq:(h + 1) * dec_seq, h * d:(h + 1) * d]


def _sb_sample(q_s, k_new, v_new, cache_k, cache_v, page_table, sb_bias):
    db, tn, w = q_s.shape
    n_pages = page_table.shape[1]
    n_pool, page = cache_k.shape[:2]
    nh, d = SB_HEADS, SB_HEAD_DIM
    npg = SB_SAMPLE_PAGES
    assert n_pages % npg == 0 and tn <= page
    n_groups = n_pages // npg
    n_rows = nh * tn
    qh = q_s.reshape(db, tn, nh, d).transpose(0, 2, 1, 3)
    eye = jnp.eye(nh, dtype=F32)
    qbd = (qh[:, :, :, None, :] * eye[None, :, None, :, None]).reshape(db, n_rows, w).astype(BF16)
    brow = jnp.repeat(sb_bias.astype(F32), tn).reshape(n_rows, 1)
    pad = ((0, 0), (0, page - tn), (0, 0))
    knew = jnp.pad(k_new, pad).astype(BF16)
    vnew = jnp.pad(v_new, pad).astype(BF16)

    def cache_map(p):
        return lambda b, g, pt: (pt[b * n_pages + n_pages - jnp.maximum(g, 1) * npg + p], 0, 0, 0)

    per_b = lambda b, g, pt: (b, 0, 0)
    const = lambda b, g, pt: (0, 0)
    grid_spec = pltpu.PrefetchScalarGridSpec(
        num_scalar_prefetch=1,
        grid=(db, n_groups + 1),
        in_specs=[pl.BlockSpec((None, n_rows, w), per_b),
                  pl.BlockSpec((n_rows, 1), const),
                  pl.BlockSpec((None, page, w), per_b),
                  pl.BlockSpec((None, page, w), per_b)]
        + [pl.BlockSpec((None, page, nh, d), cache_map(p)) for p in range(npg)] * 2,
        out_specs=pl.BlockSpec((None, tn, w), per_b),
        scratch_shapes=[pltpu.VMEM((n_rows, w), F32), pltpu.VMEM((n_rows, 1), F32)],
    )
    return pl.pallas_call(
        functools.partial(_sb_sample_kernel, n_groups=n_groups, dec_seq=tn),
        grid_spec=grid_spec,
        out_shape=jax.ShapeDtypeStruct((db, tn, w), F32),
        compiler_params=_cparams(("parallel", "arbitrary")),
        name="sb_sample",
    )(page_table.reshape(-1).astype(jnp.int32), qbd, brow, knew, vnew,
      *([cache_k] * npg), *([cache_v] * npg))


def _ret_tables(positions, chunk_len, tile):
    half = RET_DK // 2
    inv = np.float32(ROPE_BASE) ** (-(np.arange(half, dtype=np.float32) / np.float32(half)))
    ang = positions.astype(np.float32)[:, None] * inv.astype(np.float32)[None, :]
    c = np.cos(ang.astype(np.float64)).astype(np.float32)
    s = np.sin(ang.astype(np.float64)).astype(np.float32)
    cos2 = np.concatenate([c, c], axis=1)
    sin2 = np.concatenate([-s, s], axis=1)
    hh = np.arange(RET_HEADS, dtype=np.float32)
    log_g = np.log(np.float32(1.0) - np.float32(2.0) ** (np.float32(-5.0) - hh)).astype(np.float32)
    i = np.arange(tile, dtype=np.float32)
    live = (np.arange(tile) < chunk_len)
    diff = i[:, None] - i[None, :]
    decay = np.where((diff >= 0)[None] & live[None, :, None] & live[None, None, :],
                     np.exp(np.maximum(diff, 0.0)[None] * log_g[:, None, None]), 0.0).astype(np.float32)
    q_dec = np.where(live[None], np.exp((i + 1.0)[None] * log_g[:, None]), 0.0).astype(np.float32)
    k_dec = np.where(live[None], np.exp((np.float32(chunk_len) - 1.0 - i)[None] * log_g[:, None]),
                     0.0).astype(np.float32)
    c_dec = np.exp(np.float32(chunk_len) * log_g).astype(np.float32)
    k_scale = np.float32(RET_DK) ** np.float32(-0.5)
    qdec_full = np.repeat(q_dec.T, RET_DK, axis=1)
    kdec_full = np.repeat(k_dec.T, RET_DK, axis=1)
    cdec_full = np.broadcast_to(c_dec[:, None, None], (RET_HEADS, 1, RET_DV)).copy()
    return (jnp.asarray(cos2), jnp.asarray(sin2), jnp.asarray(decay), jnp.asarray(qdec_full),
            jnp.asarray(kdec_full), jnp.asarray(cdec_full), float(k_scale))


def _retention_kernel(rq_ref, rk_ref, rv_ref, rg_ref, cos_ref, sin_ref, decay_ref, qdec_ref, kdec_ref,
                      cdec_ref, gn_ref, s0_ref, ret_ref, sfin_ref, state_ref, *, hp, k_scale):
    c = pl.program_id(1)

    @pl.when(c == 0)
    def _():
        state_ref[...] = s0_ref[...]

    cos = cos_ref[...]
    sin = sin_ref[...]

    def rope(x):
        return x * cos + pltpu.roll(x, RET_DK // 2, 1) * sin

    for h in range(RET_HEADS):
        ks = slice(h * RET_DK, (h + 1) * RET_DK)
        vs = slice(h * RET_DV, (h + 1) * RET_DV)
        q = rope(rq_ref[:, ks].astype(F32))
        k = rope(rk_ref[:, ks].astype(F32)) * k_scale
        v = rv_ref[:, vs]
        scores = _dot(q, k, hp, _NT) * decay_ref[h]
        intra = _dot(scores, v, hp)
        state = state_ref[h]
        inter = _dot(q * qdec_ref[:, ks], state, hp)
        state_ref[h] = state * cdec_ref[h] + _dot(k * kdec_ref[:, ks], v, hp, _TN)
        o = intra + inter
        mu = jnp.mean(o, axis=-1, keepdims=True)
        cen = o - mu
        var = jnp.mean(cen * cen, axis=-1, keepdims=True)
        g = rg_ref[:, vs].astype(F32)
        ret = cen * lax.rsqrt(var + GN_EPS) * gn_ref[:, vs] * (g * jax.nn.sigmoid(g))
        ret_ref[:, vs] = ret.astype(ret_ref.dtype)

    @pl.when(c == pl.num_programs(1) - 1)
    def _():
        sfin_ref[...] = state_ref[...]


def _retention(rq, rk, rv, rg, gn_g, s0, batch, n_chunks, tables, *, hp, out_dtype):
    cos2, sin2, decay, qdec, kdec, cdec, k_scale = tables
    tile = decay.shape[1]
    wk = RET_HEADS * RET_DK
    wv = RET_HEADS * RET_DV
    (rq, cq), (rk, ck), (rv, cv), (rg, cg) = rq, rk, rv, rg
    row = lambda b, c: (b * n_chunks + c, 0)
    rowc = lambda cb: (lambda b, c: (b * n_chunks + c, cb))
    const2 = lambda b, c: (0, 0)
    const3 = lambda b, c: (0, 0, 0)
    return pl.pallas_call(
        functools.partial(_retention_kernel, hp=hp, k_scale=k_scale),
        grid=(batch, n_chunks),
        in_specs=[pl.BlockSpec((tile, wk), rowc(cq)), pl.BlockSpec((tile, wk), rowc(ck)),
                  pl.BlockSpec((tile, wv), rowc(cv)), pl.BlockSpec((tile, wv), rowc(cg)),
                  pl.BlockSpec((tile, RET_DK), lambda b, c: (c, 0)),
                  pl.BlockSpec((tile, RET_DK), lambda b, c: (c, 0)),
                  pl.BlockSpec(decay.shape, const3),
                  pl.BlockSpec(qdec.shape, const2), pl.BlockSpec(kdec.shape, const2),
                  pl.BlockSpec(cdec.shape, const3),
                  pl.BlockSpec((1, wv), const2),
                  pl.BlockSpec((None, RET_HEADS, RET_DK, RET_DV), lambda b, c: (b, 0, 0, 0))],
        out_specs=[pl.BlockSpec((tile, wv), row),
                   pl.BlockSpec((None, RET_HEADS, RET_DK, RET_DV), lambda b, c: (b, 0, 0, 0))],
        out_shape=[jax.ShapeDtypeStruct((batch * n_chunks * tile, wv), out_dtype),
                   jax.ShapeDtypeStruct((batch, RET_HEADS, RET_DK, RET_DV), F32)],
        scratch_shapes=[pltpu.VMEM((RET_HEADS, RET_DK, RET_DV), F32)],
        compiler_params=_cparams(("parallel", "arbitrary")),
        name="retention_hp" if hp else "retention",
    )(rq, rk, rv, rg, cos2, sin2, decay, qdec, kdec, cdec, gn_g.reshape(1, wv).astype(F32), s0)


def _merge_kernel(osb_ref, ret_ref, wa_ref, wb_ref, ga_ref, gb_ref, m_ref, *scratch, hp):
    if hp:
        pa = _dot(osb_ref[...], wa_ref[...], True)
        pb = _dot(ret_ref[...], wb_ref[...], True)
    else:
        wa16_ref, wb16_ref = scratch

        @pl.when(pl.program_id(1) == 0)
        def _():
            wa16_ref[...] = wa_ref[...].astype(BF16)
            wb16_ref[...] = wb_ref[...].astype(BF16)

        pa = jnp.dot(osb_ref[...], wa16_ref[...], preferred_element_type=F32)
        pb = jnp.dot(ret_ref[...], wb16_ref[...], preferred_element_type=F32)
    m = jax.nn.sigmoid(ga_ref[...].astype(F32)) * pa + jax.nn.sigmoid(gb_ref[...].astype(F32)) * pb
    m_ref[...] = m.astype(m_ref.dtype)


def _merge(osb, ret, wa, wb, gab, *, hp, out_dtype, tm=1024, tn=512):
    m, k = osb.shape
    n = wa.shape[1]
    tm = min(tm, m)
    nb = n // tn
    row = pl.BlockSpec((tm, k), lambda j, i: (i, 0))
    wsp = pl.BlockSpec((k, tn), lambda j, i: (0, j))
    tile = pl.BlockSpec((tm, tn), lambda j, i: (i, j))
    tile_b = pl.BlockSpec((tm, tn), lambda j, i: (i, nb + j))
    return pl.pallas_call(
        functools.partial(_merge_kernel, hp=hp),
        grid=(nb, m // tm),
        in_specs=[row, row, wsp, wsp, tile, tile_b],
        out_specs=tile,
        out_shape=jax.ShapeDtypeStruct((m, n), out_dtype),
        scratch_shapes=[] if hp else [pltpu.VMEM((k, tn), BF16), pltpu.VMEM((k, tn), BF16)],
        compiler_params=_cparams(("parallel", "arbitrary")),
        name="merge_hp" if hp else "merge",
    )(osb, ret, wa, wb, gab, gab)


def _layer_norm(x, g, b):
    mu = jnp.mean(x, axis=-1, keepdims=True)
    cen = x - mu
    var = jnp.mean(cen * cen, axis=-1, keepdims=True)
    return cen * lax.rsqrt(var + LN_EPS) * g + b


def _route(logits):
    lane = lax.broadcasted_iota(jnp.int32, logits.shape, 1)
    big = jnp.int32(ROUTE_LANES)
    neg = -jnp.inf

    def first_max(x):
        m = jnp.max(x, axis=1, keepdims=True)
        idx = jnp.min(jnp.where(x == m, lane, big), axis=1, keepdims=True)
        return m, idx

    gl = jnp.where(lane < N_GROUPS, logits, neg)
    gmax, gidx = first_max(gl)
    g_weight = 1.0 / jnp.sum(jnp.exp(gl - gmax), axis=1, keepdims=True)
    lo = N_GROUPS + gidx * EXPERTS_PER_GROUP
    el = jnp.where((lane >= lo) & (lane < lo + EXPERTS_PER_GROUP), logits, neg)
    v0, i0 = first_max(el)
    v1, i1 = first_max(jnp.where(lane == i0, neg, el))
    e = jnp.exp(v1 - v0)
    w0 = 1.0 / (1.0 + e)
    w1 = e / (1.0 + e)
    vals = [(i0 - N_GROUPS).astype(F32), (i1 - N_GROUPS).astype(F32), g_weight * w0, g_weight * w1]
    out = jnp.zeros(logits.shape, F32)
    for n, val in enumerate(vals):
        out = jnp.where(lane == n, val, out)
    return out


def _rows_to_tiles(x):
    t, w = x.shape
    return x.reshape(t, w // LANES, LANES)


def _tiles_to_rows(x3):
    x = jnp.swapaxes(x3, 0, 1)
    return jnp.concatenate([x[c] for c in range(x.shape[0])], axis=1)


def _out_ln_kernel(m_ref, wo_ref, x_ref, g_ref, b_ref, wr_ref, br_ref, h_ref, h3_ref, route_ref,
                   *scratch, hp, alpha, n_real):
    i = pl.program_id(0)
    if not hp:
        (wo16_ref,) = scratch

        @pl.when(i == 0)
        def _():
            wo16_ref[...] = wo_ref[...].astype(BF16)

    @pl.when(i < n_real)
    def _():
        if hp:
            mix = _dot(m_ref[...], wo_ref[...], True)
        else:
            mix = jnp.dot(m_ref[...], wo16_ref[...], preferred_element_type=F32)
        h = _layer_norm(alpha * x_ref[...] + mix, g_ref[...], b_ref[...])
        h_ref[...] = h
        h3_ref[...] = _rows_to_tiles(h.astype(BF16))
        logits = _dot(h, wr_ref[...], True) + br_ref[...]
        route_ref[...] = _route(logits)

    @pl.when(i >= n_real)
    def _():
        h3_ref[...] = jnp.zeros_like(h3_ref)


def _out_ln(m, w_out, x, ln_g, ln_b, w_route, b_route, *, hp, alpha, spare_tiles=0, tm=256):
    t, d = x.shape
    tm = min(tm, t)
    n_real = t // tm
    const = lambda i: (0, 0)
    row = lambda i: (jnp.minimum(i, n_real - 1), 0)
    return pl.pallas_call(
        functools.partial(_out_ln_kernel, hp=hp, alpha=alpha, n_real=n_real),
        grid=(n_real + spare_tiles,),
        in_specs=[pl.BlockSpec((tm, d), row),
                  pl.BlockSpec((d, d), const, pipeline_mode=pl.Buffered(1)),
                  pl.BlockSpec((tm, d), row),
                  pl.BlockSpec((1, d), const), pl.BlockSpec((1, d), const),
                  pl.BlockSpec((d, ROUTE_LANES), const), pl.BlockSpec((1, ROUTE_LANES), const)],
        out_specs=[pl.BlockSpec((tm, d), row),
                   pl.BlockSpec((tm, d // LANES, LANES), lambda i: (i, 0, 0)),
                   pl.BlockSpec((tm, ROUTE_LANES), row)],
        out_shape=[jax.ShapeDtypeStruct((t, d), F32),
                   jax.ShapeDtypeStruct((t + spare_tiles * tm, d // LANES, LANES), BF16),
                   jax.ShapeDtypeStruct((t, ROUTE_LANES), F32)],
        scratch_shapes=[] if hp else [pltpu.VMEM((d, d), BF16)],
        compiler_params=_cparams(("arbitrary",)),
        name="out_ln_hp" if hp else "out_ln",
    )(m, w_out, x, ln_g.reshape(1, d), ln_b.reshape(1, d), w_route, b_route)


def _row_copy(src_hbm, row, dst, dst_row, sem):
    return pltpu.make_async_copy(src_hbm.at[pl.ds(row, 1)], dst.at[pl.ds(dst_row, 1)], sem)


def _gather_rows(src_hbm, idx_ref, tile, n, dst, sem, wait):
    for r in range(n):
        cp = _row_copy(src_hbm, idx_ref[tile, r], dst, r, sem)
        if wait:
            cp.wait()
        else:
            cp.start(priority=r % 2)


def _dispatch_kernel(idx_ref, h3_ref, xs_in_hbm, xs_hbm, sem):
    del xs_in_hbm
    t = pl.program_id(0)
    tm = h3_ref.shape[0]
    for r in range(2 * tm):
        _row_copy(h3_ref, r % tm, xs_hbm, idx_ref[t, r], sem.at[0]).start(priority=r % 2)
    for r in range(2 * tm):
        _row_copy(h3_ref, r % tm, xs_hbm, 0, sem.at[0]).wait()


def _dispatch(h3, tok_slots, xs, *, tm):
    t, nc, _ = h3.shape
    grid_spec = pltpu.PrefetchScalarGridSpec(
        num_scalar_prefetch=1,
        grid=(t // tm,),
        in_specs=[pl.BlockSpec((tm, nc, LANES), lambda i, ix: (i, 0, 0)),
                  pl.BlockSpec(memory_space=pl.ANY)],
        out_specs=pl.BlockSpec(memory_space=pl.ANY),
        scratch_shapes=[pltpu.SemaphoreType.DMA((1,))],
    )
    return pl.pallas_call(
        _dispatch_kernel,
        grid_spec=grid_spec,
        out_shape=jax.ShapeDtypeStruct(xs.shape, xs.dtype),
        input_output_aliases={2: 0},
        compiler_params=_cparams(("arbitrary",)),
        name="dispatch",
    )(tok_slots, h3, xs)


def _moe_ffn_kernel(te_ref, nt_ref, x_ref, w1_ref, w3_ref, w2_ref, y_ref):
    t = pl.program_id(0)
    n_tiles = nt_ref[0]

    @pl.when(t < n_tiles)
    def _():
        x = _tiles_to_rows(x_ref[...])
        a = jnp.dot(x, w1_ref[...].astype(BF16), preferred_element_type=F32)
        c = jnp.dot(x, w3_ref[...].astype(BF16), preferred_element_type=F32)
        hid = (a * jax.nn.sigmoid(a)) * c
        y = jnp.dot(hid.astype(BF16), w2_ref[...].astype(BF16), preferred_element_type=F32)
        y_ref[...] = _rows_to_tiles(y.astype(BF16))

    @pl.when(t >= n_tiles)
    def _():
        y_ref[...] = jnp.zeros_like(y_ref)


def _moe_ffn(xs, tile_expert, n_tiles, w1, w3, w2):
    nc = xs.shape[1]
    nt = xs.shape[0] // MOE_TILE
    d = nc * LANES
    ff = w1.shape[2]
    grid_spec = pltpu.PrefetchScalarGridSpec(
        num_scalar_prefetch=2,
        grid=(nt,),
        in_specs=[
            pl.BlockSpec((MOE_TILE, nc, LANES), lambda t, te, nv: (jnp.minimum(t, nv[0] - 1), 0, 0)),
            pl.BlockSpec((None, d, ff), lambda t, te, nv: (te[t], 0, 0)),
            pl.BlockSpec((None, d, ff), lambda t, te, nv: (te[t], 0, 0)),
            pl.BlockSpec((None, ff, d), lambda t, te, nv: (te[t], 0, 0)),
        ],
        out_specs=pl.BlockSpec((MOE_TILE, nc, LANES), lambda t, te, nv: (t, 0, 0)),
    )
    return pl.pallas_call(
        _moe_ffn_kernel,
        grid_spec=grid_spec,
        out_shape=jax.ShapeDtypeStruct((nt * MOE_TILE, nc, LANES), BF16),
        compiler_params=_cparams(("arbitrary",)),
        name="moe_ffn",
    )(tile_expert, n_tiles, xs, w1, w3, w2)


def _combine_ln_kernel(idx_ref, h_ref, route_ref, g_ref, b_ref, y_hbm, o_ref, ybuf, sem, *, alpha):
    t = pl.program_id(0)
    nt = pl.num_programs(0)
    slot = t % 2
    rows = ybuf.shape[1]
    nxt = jnp.minimum(t + 1, nt - 1)

    @pl.when(t == 0)
    def _():
        _gather_rows(y_hbm, idx_ref, 0, rows, ybuf.at[0], sem.at[0], wait=False)

    _gather_rows(y_hbm, idx_ref, t, rows, ybuf.at[slot], sem.at[slot], wait=True)
    tm = rows // 2
    ya = _tiles_to_rows(ybuf[slot, pl.ds(0, tm)]).astype(F32)
    yb = _tiles_to_rows(ybuf[slot, pl.ds(tm, tm)]).astype(F32)
    _gather_rows(y_hbm, idx_ref, nxt, rows, ybuf.at[1 - slot], sem.at[1 - slot], wait=False)
    route = route_ref[...]
    moe = route[:, 2:3] * ya + route[:, 3:4] * yb
    o_ref[...] = _layer_norm(alpha * h_ref[...] + moe, g_ref[...], b_ref[...])

    @pl.when(t == nt - 1)
    def _():
        _gather_rows(y_hbm, idx_ref, nxt, rows, ybuf.at[1 - slot], sem.at[1 - slot], wait=True)


def _combine_ln(h, n_rows, tok_slots, route, y_sorted, ln_g, ln_b, *, alpha, tm):
    d = h.shape[1]
    nc = y_sorted.shape[1]
    nt = n_rows // tm
    const = lambda t, ix: (0, 0)
    row = lambda t, ix: (t, 0)
    grid_spec = pltpu.PrefetchScalarGridSpec(
        num_scalar_prefetch=1,
        grid=(nt,),
        in_specs=[pl.BlockSpec((tm, d), row),
                  pl.BlockSpec((tm, ROUTE_LANES), row),
                  pl.BlockSpec((1, d), const), pl.BlockSpec((1, d), const),
                  pl.BlockSpec(memory_space=pl.ANY)],
        out_specs=pl.BlockSpec((tm, d), row),
        scratch_shapes=[pltpu.VMEM((2, 2 * tm, nc, LANES), BF16), pltpu.SemaphoreType.DMA((2,))],
    )
    return pl.pallas_call(
        functools.partial(_combine_ln_kernel, alpha=alpha),
        grid_spec=grid_spec,
        out_shape=jax.ShapeDtypeStruct((n_rows, d), F32),
        compiler_params=_cparams(("arbitrary",)),
        name="combine_ln",
    )(tok_slots, h, route, ln_g.reshape(1, d), ln_b.reshape(1, d), y_sorted)


def _moe_plan(route, n_tokens):
    e = jnp.concatenate([route[:, 0], route[:, 1]]).astype(jnp.int32)
    n_assign = 2 * n_tokens
    nt = (n_assign + N_EXPERTS * (MOE_TILE - 1)) // MOE_TILE
    onehot = (e[:, None] == jnp.arange(N_EXPERTS, dtype=jnp.int32)[None, :]).astype(jnp.int32)
    csum = jnp.cumsum(onehot, axis=0)
    counts = csum[-1]
    rank = jnp.take_along_axis(csum, e[:, None], axis=1)[:, 0] - 1
    tiles_per_e = (counts + MOE_TILE - 1) // MOE_TILE
    tile_end = jnp.cumsum(tiles_per_e)
    tile_start = tile_end - tiles_per_e
    n_tiles = tile_end[-1:]
    slot = tile_start[e] * MOE_TILE + rank
    tile_ids = jnp.minimum(jnp.arange(nt, dtype=jnp.int32), n_tiles[0] - 1)
    tile_expert = jnp.minimum(jnp.sum((tile_end[None, :] <= tile_ids[:, None]).astype(jnp.int32), axis=1),
                              N_EXPERTS - 1)
    return tile_expert, n_tiles.astype(jnp.int32), nt * MOE_TILE, slot[:n_tokens], slot[n_tokens:]


def _tile_slots(slot_a, slot_b, tm):
    n = slot_a.shape[0] // tm
    return jnp.concatenate([slot_a.reshape(n, tm), slot_b.reshape(n, tm)], axis=1)


def kernel(x_prompt, x_sample, cache_k, cache_v, state_ret, page_table, w_in, sb_bias, w_branch_a, w_branch_b,
           w_out, gn_g, ln1_g, ln1_b, ln2_g, ln2_b, w_group_router, b_group_router, w_expert_router,
           b_expert_router, w1, w3, w2):
    depth = w_in.shape[0]
    assert depth == 1, "single-layer step"
    batch, seq, d = x_prompt.shape
    db, tn, _ = x_sample.shape
    sb_w = SB_HEADS * SB_HEAD_DIM
    qk_w = RET_HEADS * RET_DK
    v_w = RET_HEADS * RET_DV
    widths = (sb_w, sb_w, sb_w, qk_w, qk_w, v_w, v_w, d, d)
    offs = np.concatenate([[0], np.cumsum(widths)])
    o_q, o_k, o_rq, o_rv, o_ga, o_end = offs[0], offs[1], offs[3], offs[5], offs[7], offs[9]
    alpha = float((2.0 * depth) ** 0.25)
    sb_scale = float(SB_HEAD_DIM ** -0.5) * LOG2_E
    n_pages = page_table.shape[1]
    page = cache_k.shape[2]
    past = n_pages * page

    w_in0 = w_in[0]
    wa, wb, wo = w_branch_a[0], w_branch_b[0], w_out[0]
    bias = sb_bias[0] * LOG2_E
    gn = gn_g[0]
    w_route = jnp.concatenate(
        [w_group_router[0], w_expert_router[0].reshape(d, N_EXPERTS),
         jnp.zeros((d, ROUTE_LANES - N_GROUPS - N_EXPERTS), F32)], axis=1)
    b_route = jnp.concatenate(
        [b_group_router[0], b_expert_router[0].reshape(N_EXPERTS),
         jnp.zeros((ROUTE_LANES - N_GROUPS - N_EXPERTS,), F32)]).reshape(1, ROUTE_LANES)

    t_p = batch * seq
    t_s = db * tn
    t_all = t_p + t_s

    xp = x_prompt.reshape(t_p, d)
    xp16 = xp.astype(BF16)
    (q16,) = _mm(xp16, w_in0, o_q, sb_w, (BF16,), scale=sb_scale)
    k32, k16 = _mm(xp16, w_in0, o_k, sb_w, (F32, BF16))
    v32, v16 = _mm(xp16, w_in0, o_k + sb_w, sb_w, (F32, BF16))
    (rqk,) = _mm(xp16, w_in0, o_rq, 2 * qk_w, (F32,))
    (rvg,) = _mm(xp16, w_in0, o_rv, 2 * v_w, (BF16,))
    (gab,) = _mm(xp16, w_in0, o_ga, 2 * d, (BF16,))

    o_sb = _sb_prompt(q16, k16, v16, bias, batch, seq)

    n_chunks = seq // RET_CHUNK
    tables_p = _ret_tables(np.arange(seq), RET_CHUNK, RET_CHUNK)
    s0_p = jnp.zeros((batch, RET_HEADS, RET_DK, RET_DV), F32)
    ret_p, st_p = _retention((rqk, 0), (rqk, 1), (rvg, 0), (rvg, 1), gn, s0_p, batch, n_chunks, tables_p,
                             hp=False, out_dtype=BF16)
    m_p = _merge(o_sb, ret_p, wa, wb, gab, hp=False, out_dtype=BF16)
    tm_h = 256
    assert t_s <= tm_h
    h_p, h3_p, route_p = _out_ln(m_p, wo, xp, ln1_g[0], ln1_b[0], w_route, b_route,
                                 hp=False, alpha=alpha, tm=tm_h)

    xs = x_sample.reshape(t_s, d)
    (p_s,) = _mm(xs, w_in0, 0, int(o_end), (F32,), hp=True, tn=512)
    k_s = p_s[:, o_k:o_k + sb_w]
    v_s = p_s[:, o_k + sb_w:o_rq]
    q_s = p_s[:, o_q:o_k] * sb_scale
    o_sb_s = _sb_sample(q_s.reshape(db, tn, sb_w), k_s.reshape(db, tn, sb_w), v_s.reshape(db, tn, sb_w),
                        cache_k[0], cache_v[0], page_table, bias)

    def pad_rows(a):
        return jnp.pad(a.reshape(db, tn, -1), ((0, 0), (0, RET_CHUNK - tn), (0, 0))).reshape(db * RET_CHUNK, -1)

    pos_s = past + np.arange(RET_CHUNK)
    tables_s = _ret_tables(pos_s, tn, RET_CHUNK)
    rqk_s = pad_rows(p_s[:, o_rq:o_rv])
    rvg_s = pad_rows(p_s[:, o_rv:o_ga])
    ret_s_pad, st_s = _retention((rqk_s, 0), (rqk_s, 1), (rvg_s, 0), (rvg_s, 1), gn, state_ret[0], db, 1,
                                 tables_s, hp=True, out_dtype=F32)
    ret_s = ret_s_pad.reshape(db, RET_CHUNK, v_w)[:, :tn].reshape(t_s, v_w)
    m_s = _merge(o_sb_s.reshape(t_s, sb_w), ret_s, wa, wb, p_s[:, o_ga:o_end], hp=True, out_dtype=F32)
    h_s, h3_s, route_s = _out_ln(m_s, wo, xs, ln1_g[0], ln1_b[0], w_route, b_route, hp=True, alpha=alpha)

    route = jnp.concatenate([route_p, route_s], axis=0)
    tile_expert, n_tiles, n_slots, slot_a, slot_b = _moe_plan(route, t_all)
    tm_p = 256
    slots_p = _tile_slots(slot_a[:t_p], slot_b[:t_p], tm_p)
    slots_s = _tile_slots(slot_a[t_p:], slot_b[t_p:], t_s)
    x_sorted = jnp.zeros((n_slots, d // LANES, LANES), BF16)
    x_sorted = _dispatch(h3_p, slots_p, x_sorted, tm=tm_p)
    x_sorted = _dispatch(h3_s, slots_s, x_sorted, tm=t_s)
    y_sorted = _moe_ffn(x_sorted, tile_expert, n_tiles, w1[0], w3[0], w2[0])
    y_p = _combine_ln(h_p, t_p, slots_p, route_p, y_sorted, ln2_g[0], ln2_b[0], alpha=alpha, tm=tm_p)
    y_s = _combine_ln(h_s, t_s, slots_s, route_s, y_sorted, ln2_g[0], ln2_b[0], alpha=alpha, tm=t_s)

    hs = (SB_HEADS, SB_HEAD_DIM)
    return (y_p.reshape(batch, seq, d), y_s.reshape(db, tn, d),
            k32.reshape(1, batch, seq, *hs), v32.reshape(1, batch, seq, *hs),
            st_p[None],
            k_s.reshape(1, db, tn, *hs), v_s.reshape(1, db, tn, *hs),
            st_s[None])
```

```python
import functools

import numpy as np
import jax
import jax.numpy as jnp
from jax import lax
from jax.experimental import pallas as pl
from jax.experimental.pallas import tpu as pltpu

F32 = jnp.float32
BF16 = jnp.bfloat16

SB_HEADS = 16
SB_HEAD_DIM = 128
RET_HEADS = 8
RET_DK = 128
RET_DV = 256
RET_CHUNK = 128
ROPE_BASE = 10000.0
N_GROUPS = 4
EXPERTS_PER_GROUP = 8
N_EXPERTS = N_GROUPS * EXPERTS_PER_GROUP
LN_EPS = 1e-5
GN_EPS = 1e-5

V7X_VMEM_LIMIT_BYTES = 56 * 1024 * 1024
LANES = 128
MOE_TILE = 256
ROUTE_LANES = LANES


def _cparams(sem):
    return pltpu.CompilerParams(dimension_semantics=sem, vmem_limit_bytes=V7X_VMEM_LIMIT_BYTES)


def _split_hi_lo(a):
    hi = a.astype(BF16)
    lo = (a - hi.astype(F32)).astype(BF16)
    return hi, lo


def _dot(a, b, hp, dims=(((1,), (0,)), ((), ()))):
    if not hp:
        return lax.dot_general(a.astype(BF16), b.astype(BF16), dims, preferred_element_type=F32)
    ah, al = _split_hi_lo(a.astype(F32))
    bh, bl = _split_hi_lo(b.astype(F32))
    d = functools.partial(lax.dot_general, dimension_numbers=dims, preferred_element_type=F32)
    return d(ah, bh) + (d(ah, bl) + d(al, bh))


_NT = (((1,), (1,)), ((), ()))
_TN = (((0,), (0,)), ((), ()))


def _mm_kernel(x_ref, w_ref, *refs, hp, scale):
    if hp:
        out_refs = refs
        acc = _dot(x_ref[...], w_ref[...], True)
    else:
        *out_refs, w16_ref = refs

        @pl.when(pl.program_id(1) == 0)
        def _():
            w16_ref[...] = w_ref[...].astype(BF16)

        acc = jnp.dot(x_ref[...], w16_ref[...], preferred_element_type=F32)
    if scale is not None:
        acc = acc * scale
    for o in out_refs:
        o[...] = acc.astype(o.dtype)


def _mm(x, w, col0, n, out_dtypes, *, hp=False, scale=None, tm=1024, tn=1024):
    m, k = x.shape
    col0, n = int(col0), int(n)
    tm = min(tm, m)
    tn = min(tn, n)
    assert m % tm == 0 and n % tn == 0 and col0 % tn == 0
    j0 = col0 // tn
    outs = pl.pallas_call(
        functools.partial(_mm_kernel, hp=hp, scale=scale),
        grid=(n // tn, m // tm),
        in_specs=[pl.BlockSpec((tm, k), lambda j, i: (i, 0)),
                  pl.BlockSpec((k, tn), lambda j, i: (0, j0 + j))],
        out_specs=[pl.BlockSpec((tm, tn), lambda j, i: (i, j)) for _ in out_dtypes],
        out_shape=[jax.ShapeDtypeStruct((m, n), dt) for dt in out_dtypes],
        scratch_shapes=[] if hp else [pltpu.VMEM((k, tn), BF16)],
        compiler_params=_cparams(("parallel", "arbitrary")),
        name="proj_hp" if hp else "proj",
    )(x, w)
    return outs


LOG2_E = 1.4426950408889634


EXP2_MAX_ARG = 126.0


def _softplus2(z2):
    return jnp.maximum(jnp.log2(1.0 + jnp.exp2(jnp.minimum(z2, EXP2_MAX_ARG))), z2)


def _strict_lower_ones(n):
    r = lax.broadcasted_iota(jnp.int32, (n, n), 0)
    c = lax.broadcasted_iota(jnp.int32, (n, n), 1)
    return jnp.where(r > c, 1.0, 0.0).astype(BF16)


SB_SUB = 256


def _sb_logit_stage(z2, valid, sub):
    s = _softplus2(z2)
    zs = z2 - s
    if valid is not None:
        s = jnp.where(valid, s, 0.0)
        zs = jnp.where(valid, zs, -jnp.inf)
    width = z2.shape[1]
    sub = min(sub, width)
    tri = _strict_lower_ones(sub)
    tails = []
    run = None
    for c in reversed(range(width // sub)):
        sc = s[:, c * sub:(c + 1) * sub]
        t = jnp.dot(sc.astype(BF16), tri, preferred_element_type=F32)
        rs = jnp.sum(sc, axis=1, keepdims=True)
        if run is None:
            run = rs
        else:
            t = t + run
            run = run + rs
        tails.append(t)
    return zs, jnp.concatenate(tails[::-1], axis=1), run


def _sb_weights(zs, tail):
    return jnp.exp2(zs - tail).astype(BF16)


def _sb_prompt_kernel(bias_ref, q_ref, k_ref, v_ref, o_ref, acc_ref, zs_ref, tail_ref, carry_ref,
                      rowsum_ref, *, tq):
    h = pl.program_id(1)
    qi = pl.program_id(2)
    lane = lax.broadcasted_iota(jnp.int32, (tq, SB_HEAD_DIM), 1)
    b = jnp.full((tq, SB_HEAD_DIM), bias_ref[h], F32)
    p0 = b.astype(BF16).astype(F32)
    p1 = (b - p0).astype(BF16).astype(F32)
    p2 = ((b - p0) - p1).astype(BF16).astype(F32)
    q_bias = jnp.where(lane == 0, p0, jnp.where(lane == 1, p1, jnp.where(lane == 2, p2, 0.0))).astype(BF16)
    k_ones = jnp.where(lane < 3, 1.0, 0.0).astype(BF16)
    q = jnp.concatenate([q_ref[...], q_bias], axis=1)
    acc_ref[...] = jnp.zeros_like(acc_ref)
    carry_ref[...] = jnp.zeros_like(carry_ref)

    def rows(j):
        return pl.ds(pl.multiple_of(j * tq, tq), tq)

    def logits(j, masked, p):
        k = jnp.concatenate([k_ref[rows(j), :], k_ones], axis=1)
        z2 = lax.dot_general(q, k, _NT, preferred_element_type=F32)
        valid = None
        if masked:
            valid = (lax.broadcasted_iota(jnp.int32, (tq, tq), 1)
                     < lax.broadcasted_iota(jnp.int32, (tq, tq), 0))
        zs_ref[p], tail_ref[p], rowsum_ref[p] = _sb_logit_stage(z2, valid, SB_SUB)

    def weights(j, p):
        w = _sb_weights(zs_ref[p], tail_ref[p])
        carry = carry_ref[...]
        acc_ref[...] += jnp.dot(w, v_ref[rows(j), :], preferred_element_type=F32) * jnp.exp2(-carry)
        carry_ref[...] = carry + rowsum_ref[p]

    def step(m, p):
        weights(qi - m, p)
        logits(qi - m - 1, False, 1 - p)

    logits(qi, True, 0)

    def pair(t, c):
        step(2 * t, 0)
        step(2 * t + 1, 1)
        return c

    lax.fori_loop(0, qi // 2, pair, 0)

    @pl.when(qi % 2 == 1)
    def _():
        step(qi - 1, 0)

    weights(0, qi % 2)
    o_ref[...] = acc_ref[...].astype(o_ref.dtype)


def _sb_prompt(q, k, v, sb_bias, batch, seq, *, tq=1024):
    tq = min(tq, seq)
    nq = seq // tq
    d = SB_HEAD_DIM
    return pl.pallas_call(
        functools.partial(_sb_prompt_kernel, tq=tq),
        grid=(batch, SB_HEADS, nq),
        in_specs=[pl.BlockSpec(memory_space=pltpu.SMEM),
                  pl.BlockSpec((tq, d), lambda b, h, i: (b * nq + i, h)),
                  pl.BlockSpec((seq, d), lambda b, h, i: (b, h)),
                  pl.BlockSpec((seq, d), lambda b, h, i: (b, h))],
        out_specs=pl.BlockSpec((tq, d), lambda b, h, i: (b * nq + i, h)),
        out_shape=jax.ShapeDtypeStruct(q.shape, BF16),
        scratch_shapes=[pltpu.VMEM((tq, d), F32), pltpu.VMEM((2, tq, tq), F32), pltpu.VMEM((2, tq, tq), F32),
                        pltpu.VMEM((tq, 1), F32), pltpu.VMEM((2, tq, 1), F32)],
        compiler_params=_cparams(("parallel", "parallel", "arbitrary")),
        name="sb_prompt",
    )(sb_bias.astype(F32), q, k, v)


SB_SAMPLE_PAGES = 4


def _sb_sample_kernel(pt_ref, qbd_ref, brow_ref, knew_ref, vnew_ref, *refs, n_groups, dec_seq):
    npg = SB_SAMPLE_PAGES
    k_refs, v_refs = refs[:npg], refs[npg:2 * npg]
    o_ref, acc_ref, carry_ref = refs[2 * npg:]
    g = pl.program_id(1)
    qbd = qbd_ref[...]
    n_rows = qbd.shape[0]
    page = knew_ref.shape[0]

    def attend(k_all, v_all, valid):
        z2 = lax.dot_general(qbd, k_all, _NT, preferred_element_type=F32) + brow_ref[...]
        zs, tail, rowsum = _sb_logit_stage(z2, valid, SB_SUB)
        w = _sb_weights(zs, tail)
        carry = carry_ref[...]
        acc_ref[...] += jnp.dot(w, v_all, preferred_element_type=F32) * jnp.exp2(-carry)
        carry_ref[...] = carry + rowsum

    @pl.when(g == 0)
    def _():
        acc_ref[...] = jnp.zeros_like(acc_ref)
        carry_ref[...] = jnp.zeros_like(carry_ref)
        t = lax.broadcasted_iota(jnp.int32, (n_rows, page), 0) % dec_seq
        s = lax.broadcasted_iota(jnp.int32, (n_rows, page), 1)
        attend(knew_ref[...], vnew_ref[...], s < t)

    @pl.when(g > 0)
    def _():
        def rows_of(ref):
            x = jnp.swapaxes(ref[...].astype(BF16), 0, 1)
            return jnp.concatenate([x[h] for h in range(SB_HEADS)], axis=1)

        attend(jnp.concatenate([rows_of(r) for r in k_refs], axis=0),
               jnp.concatenate([rows_of(r) for r in v_refs], axis=0), None)

    @pl.when(g == n_groups)
    def _():
        d = SB_HEAD_DIM
        for h in range(SB_HEADS):
            o_ref[:, h * d:(h + 1) * d] = acc_ref[h * dec_seq:(h + 1) * dec_seq, h * d:(h + 1) * d]


def _sb_sample(q_s, k_new, v_new, cache_k, cache_v, page_table, sb_bias):
    db, tn, w = q_s.shape
    n_pages = page_table.shape[1]
    n_pool, page = cache_k.shape[:2]
    nh, d = SB_HEADS, SB_HEAD_DIM
    npg = SB_SAMPLE_PAGES
    assert n_pages % npg == 0 and tn <= page
    n_groups = n_pages // npg
    n_rows = nh * tn
    qh = q_s.reshape(db, tn, nh, d).transpose(0, 2, 1, 3)
    eye = jnp.eye(nh, dtype=F32)
    qbd = (qh[:, :, :, None, :] * eye[None, :, None, :, None]).reshape(db, n_rows, w).astype(BF16)
    brow = jnp.repeat(sb_bias.astype(F32), tn).reshape(n_rows, 1)
    pad = ((0, 0), (0, page - tn), (0, 0))
    knew = jnp.pad(k_new, pad).astype(BF16)
    vnew = jnp.pad(v_new, pad).astype(BF16)

    def cache_map(p):
        return lambda b, g, pt: (pt[b * n_pages + n_pages - jnp.maximum(g, 1) * npg + p], 0, 0, 0)

    per_b = lambda b, g, pt: (b, 0, 0)
    const = lambda b, g, pt: (0, 0)
    grid_spec = pltpu.PrefetchScalarGridSpec(
        num_scalar_prefetch=1,
        grid=(db, n_groups + 1),
        in_specs=[pl.BlockSpec((None, n_rows, w), per_b),
                  pl.BlockSpec((n_rows, 1), const),
                  pl.BlockSpec((None, page, w), per_b),
                  pl.BlockSpec((None, page, w), per_b)]
        + [pl.BlockSpec((None, page, nh, d), cache_map(p)) for p in range(npg)] * 2,
        out_specs=pl.BlockSpec((None, tn, w), per_b),
        scratch_shapes=[pltpu.VMEM((n_rows, w), F32), pltpu.VMEM((n_rows, 1), F32)],
    )
    return pl.pallas_call(
        functools.partial(_sb_sample_kernel, n_groups=n_groups, dec_seq=tn),
        grid_spec=grid_spec,
        out_shape=jax.ShapeDtypeStruct((db, tn, w), F32),
        compiler_params=_cparams(("parallel", "arbitrary")),
        name="sb_sample",
    )(page_table.reshape(-1).astype(jnp.int32), qbd, brow, knew, vnew,
      *([cache_k] * npg), *([cache_v] * npg))


def _ret_tables(positions, chunk_len, tile):
    half = RET_DK // 2
    inv = np.float32(ROPE_BASE) ** (-(np.arange(half, dtype=np.float32) / np.float32(half)))
    ang = positions.astype(np.float32)[:, None] * inv.astype(np.float32)[None, :]
    c = np.cos(ang.astype(np.float64)).astype(np.float32)
    s = np.sin(ang.astype(np.float64)).astype(np.float32)
    cos2 = np.concatenate([c, c], axis=1)
    sin2 = np.concatenate([-s, s], axis=1)
    hh = np.arange(RET_HEADS, dtype=np.float32)
    log_g = np.log(np.float32(1.0) - np.float32(2.0) ** (np.float32(-5.0) - hh)).astype(np.float32)
    i = np.arange(tile, dtype=np.float32)
    live = (np.arange(tile) < chunk_len)
    diff = i[:, None] - i[None, :]
    decay = np.where((diff >= 0)[None] & live[None, :, None] & live[None, None, :],
                     np.exp(np.maximum(diff, 0.0)[None] * log_g[:, None, None]), 0.0).astype(np.float32)
    q_dec = np.where(live[None], np.exp((i + 1.0)[None] * log_g[:, None]), 0.0).astype(np.float32)
    k_dec = np.where(live[None], np.exp((np.float32(chunk_len) - 1.0 - i)[None] * log_g[:, None]),
                     0.0).astype(np.float32)
    c_dec = np.exp(np.float32(chunk_len) * log_g).astype(np.float32)
    k_scale = np.float32(RET_DK) ** np.float32(-0.5)
    qdec_full = np.repeat(q_dec.T, RET_DK, axis=1)
    kdec_full = np.repeat(k_dec.T, RET_DK, axis=1)
    cdec_full = np.broadcast_to(c_dec[:, None, None], (RET_HEADS, 1, RET_DV)).copy()
    return (jnp.asarray(cos2), jnp.asarray(sin2), jnp.asarray(decay), jnp.asarray(qdec_full),
            jnp.asarray(kdec_full), jnp.asarray(cdec_full), float(k_scale))


def _retention_kernel(rq_ref, rk_ref, rv_ref, rg_ref, cos_ref, sin_ref, decay_ref, qdec_ref, kdec_ref,
                      cdec_ref, gn_ref, s0_ref, ret_ref, sfin_ref, state_ref, *, hp, k_scale):
    c = pl.program_id(1)

    @pl.when(c == 0)
    def _():
        state_ref[...] = s0_ref[...]

    cos = cos_ref[...]
    sin = sin_ref[...]

    def rope(x):
        return x * cos + pltpu.roll(x, RET_DK // 2, 1) * sin

    for h in range(RET_HEADS):
        ks = slice(h * RET_DK, (h + 1) * RET_DK)
        vs = slice(h * RET_DV, (h + 1) * RET_DV)
        q = rope(rq_ref[:, ks].astype(F32))
        k = rope(rk_ref[:, ks].astype(F32)) * k_scale
        v = rv_ref[:, vs]
        scores = _dot(q, k, hp, _NT) * decay_ref[h]
        intra = _dot(scores, v, hp)
        state = state_ref[h]
        inter = _dot(q * qdec_ref[:, ks], state, hp)
        state_ref[h] = state * cdec_ref[h] + _dot(k * kdec_ref[:, ks], v, hp, _TN)
        o = intra + inter
        mu = jnp.mean(o, axis=-1, keepdims=True)
        cen = o - mu
        var = jnp.mean(cen * cen, axis=-1, keepdims=True)
        g = rg_ref[:, vs].astype(F32)
        ret = cen * lax.rsqrt(var + GN_EPS) * gn_ref[:, vs] * (g * jax.nn.sigmoid(g))
        ret_ref[:, vs] = ret.astype(ret_ref.dtype)

    @pl.when(c == pl.num_programs(1) - 1)
    def _():
        sfin_ref[...] = state_ref[...]


def _retention(rq, rk, rv, rg, gn_g, s0, batch, n_chunks, tables, *, hp, out_dtype):
    cos2, sin2, decay, qdec, kdec, cdec, k_scale = tables
    tile = decay.shape[1]
    wk = RET_HEADS * RET_DK
    wv = RET_HEADS * RET_DV
    (rq, cq), (rk, ck), (rv, cv), (rg, cg) = rq, rk, rv, rg
    row = lambda b, c: (b * n_chunks + c, 0)
    rowc = lambda cb: (lambda b, c: (b * n_chunks + c, cb))
    const2 = lambda b, c: (0, 0)
    const3 = lambda b, c: (0, 0, 0)
    return pl.pallas_call(
        functools.partial(_retention_kernel, hp=hp, k_scale=k_scale),
        grid=(batch, n_chunks),
        in_specs=[pl.BlockSpec((tile, wk), rowc(cq)), pl.BlockSpec((tile, wk), rowc(ck)),
                  pl.BlockSpec((tile, wv), rowc(cv)), pl.BlockSpec((tile, wv), rowc(cg)),
                  pl.BlockSpec((tile, RET_DK), lambda b, c: (c, 0)),
                  pl.BlockSpec((tile, RET_DK), lambda b, c: (c, 0)),
                  pl.BlockSpec(decay.shape, const3),
                  pl.BlockSpec(qdec.shape, const2), pl.BlockSpec(kdec.shape, const2),
                  pl.BlockSpec(cdec.shape, const3),
                  pl.BlockSpec((1, wv), const2),
                  pl.BlockSpec((None, RET_HEADS, RET_DK, RET_DV), lambda b, c: (b, 0, 0, 0))],
        out_specs=[pl.BlockSpec((tile, wv), row),
                   pl.BlockSpec((None, RET_HEADS, RET_DK, RET_DV), lambda b, c: (b, 0, 0, 0))],
        out_shape=[jax.ShapeDtypeStruct((batch * n_chunks * tile, wv), out_dtype),
                   jax.ShapeDtypeStruct((batch, RET_HEADS, RET_DK, RET_DV), F32)],
        scratch_shapes=[pltpu.VMEM((RET_HEADS, RET_DK, RET_DV), F32)],
        compiler_params=_cparams(("parallel", "arbitrary")),
        name="retention_hp" if hp else "retention",
    )(rq, rk, rv, rg, cos2, sin2, decay, qdec, kdec, cdec, gn_g.reshape(1, wv).astype(F32), s0)


def _merge_kernel(osb_ref, ret_ref, wa_ref, wb_ref, ga_ref, gb_ref, m_ref, *scratch, hp):
    if hp:
        pa = _dot(osb_ref[...], wa_ref[...], True)
        pb = _dot(ret_ref[...], wb_ref[...], True)
    else:
        wa16_ref, wb16_ref = scratch

        @pl.when(pl.program_id(1) == 0)
        def _():
            wa16_ref[...] = wa_ref[...].astype(BF16)
            wb16_ref[...] = wb_ref[...].astype(BF16)

        pa = jnp.dot(osb_ref[...], wa16_ref[...], preferred_element_type=F32)
        pb = jnp.dot(ret_ref[...], wb16_ref[...], preferred_element_type=F32)
    m = jax.nn.sigmoid(ga_ref[...].astype(F32)) * pa + jax.nn.sigmoid(gb_ref[...].astype(F32)) * pb
    m_ref[...] = m.astype(m_ref.dtype)


def _merge(osb, ret, wa, wb, gab, *, hp, out_dtype, tm=1024, tn=512):
    m, k = osb.shape
    n = wa.shape[1]
    tm = min(tm, m)
    nb = n // tn
    row = pl.BlockSpec((tm, k), lambda j, i: (i, 0))
    wsp = pl.BlockSpec((k, tn), lambda j, i: (0, j))
    tile = pl.BlockSpec((tm, tn), lambda j, i: (i, j))
    tile_b = pl.BlockSpec((tm, tn), lambda j, i: (i, nb + j))
    return pl.pallas_call(
        functools.partial(_merge_kernel, hp=hp),
        grid=(nb, m // tm),
        in_specs=[row, row, wsp, wsp, tile, tile_b],
        out_specs=tile,
        out_shape=jax.ShapeDtypeStruct((m, n), out_dtype),
        scratch_shapes=[] if hp else [pltpu.VMEM((k, tn), BF16), pltpu.VMEM((k, tn), BF16)],
        compiler_params=_cparams(("parallel", "arbitrary")),
        name="merge_hp" if hp else "merge",
    )(osb, ret, wa, wb, gab, gab)


def _layer_norm(x, g, b):
    mu = jnp.mean(x, axis=-1, keepdims=True)
    cen = x - mu
    var = jnp.mean(cen * cen, axis=-1, keepdims=True)
    return cen * lax.rsqrt(var + LN_EPS) * g + b


def _route(logits):
    lane = lax.broadcasted_iota(jnp.int32, logits.shape, 1)
    big = jnp.int32(ROUTE_LANES)
    neg = -jnp.inf

    def first_max(x):
        m = jnp.max(x, axis=1, keepdims=True)
        idx = jnp.min(jnp.where(x == m, lane, big), axis=1, keepdims=True)
        return m, idx

    gl = jnp.where(lane < N_GROUPS, logits, neg)
    gmax, gidx = first_max(gl)
    g_weight = 1.0 / jnp.sum(jnp.exp(gl - gmax), axis=1, keepdims=True)
    lo = N_GROUPS + gidx * EXPERTS_PER_GROUP
    el = jnp.where((lane >= lo) & (lane < lo + EXPERTS_PER_GROUP), logits, neg)
    v0, i0 = first_max(el)
    v1, i1 = first_max(jnp.where(lane == i0, neg, el))
    e = jnp.exp(v1 - v0)
    w0 = 1.0 / (1.0 + e)
    w1 = e / (1.0 + e)
    vals = [(i0 - N_GROUPS).astype(F32), (i1 - N_GROUPS).astype(F32), g_weight * w0, g_weight * w1]
    out = jnp.zeros(logits.shape, F32)
    for n, val in enumerate(vals):
        out = jnp.where(lane == n, val, out)
    return out


def _rows_to_tiles(x):
    t, w = x.shape
    return x.reshape(t, w // LANES, LANES)


def _tiles_to_rows(x3):
    x = jnp.swapaxes(x3, 0, 1)
    return jnp.concatenate([x[c] for c in range(x.shape[0])], axis=1)


def _out_ln_kernel(m_ref, wo_ref, x_ref, g_ref, b_ref, wr_ref, br_ref, h_ref, h3_ref, route_ref,
                   *scratch, hp, alpha, n_real):
    i = pl.program_id(0)
    if not hp:
        (wo16_ref,) = scratch

        @pl.when(i == 0)
        def _():
            wo16_ref[...] = wo_ref[...].astype(BF16)

    @pl.when(i < n_real)
    def _():
        if hp:
            mix = _dot(m_ref[...], wo_ref[...], True)
        else:
            mix = jnp.dot(m_ref[...], wo16_ref[...], preferred_element_type=F32)
        h = _layer_norm(alpha * x_ref[...] + mix, g_ref[...], b_ref[...])
        h_ref[...] = h
        h3_ref[...] = _rows_to_tiles(h.astype(BF16))
        logits = _dot(h, wr_ref[...], True) + br_ref[...]
        route_ref[...] = _route(logits)

    @pl.when(i >= n_real)
    def _():
        h3_ref[...] = jnp.zeros_like(h3_ref)


def _out_ln(m, w_out, x, ln_g, ln_b, w_route, b_route, *, hp, alpha, spare_tiles=0, tm=256):
    t, d = x.shape
    tm = min(tm, t)
    n_real = t // tm
    const = lambda i: (0, 0)
    row = lambda i: (jnp.minimum(i, n_real - 1), 0)
    return pl.pallas_call(
        functools.partial(_out_ln_kernel, hp=hp, alpha=alpha, n_real=n_real),
        grid=(n_real + spare_tiles,),
        in_specs=[pl.BlockSpec((tm, d), row),
                  pl.BlockSpec((d, d), const, pipeline_mode=pl.Buffered(1)),
                  pl.BlockSpec((tm, d), row),
                  pl.BlockSpec((1, d), const), pl.BlockSpec((1, d), const),
                  pl.BlockSpec((d, ROUTE_LANES), const), pl.BlockSpec((1, ROUTE_LANES), const)],
        out_specs=[pl.BlockSpec((tm, d), row),
                   pl.BlockSpec((tm, d // LANES, LANES), lambda i: (i, 0, 0)),
                   pl.BlockSpec((tm, ROUTE_LANES), row)],
        out_shape=[jax.ShapeDtypeStruct((t, d), F32),
                   jax.ShapeDtypeStruct((t + spare_tiles * tm, d // LANES, LANES), BF16),
                   jax.ShapeDtypeStruct((t, ROUTE_LANES), F32)],
        scratch_shapes=[] if hp else [pltpu.VMEM((d, d), BF16)],
        compiler_params=_cparams(("arbitrary",)),
        name="out_ln_hp" if hp else "out_ln",
    )(m, w_out, x, ln_g.reshape(1, d), ln_b.reshape(1, d), w_route, b_route)


def _row_copy(src_hbm, row, dst, dst_row, sem):
    return pltpu.make_async_copy(src_hbm.at[pl.ds(row, 1)], dst.at[pl.ds(dst_row, 1)], sem)


def _gather_rows(src_hbm, idx_ref, tile, n, dst, sem, wait):
    for r in range(n):
        cp = _row_copy(src_hbm, idx_ref[tile, r], dst, r, sem)
        if wait:
            cp.wait()
        else:
            cp.start(priority=r % 2)


def _zero_tiles_kernel(o_ref):
    o_ref[...] = jnp.zeros_like(o_ref)


def _zero_tiles(n, nc, *, tm=1024):
    assert n % tm == 0
    return pl.pallas_call(
        _zero_tiles_kernel,
        grid=(n // tm,),
        out_specs=pl.BlockSpec((tm, nc, LANES), lambda i: (i, 0, 0)),
        out_shape=jax.ShapeDtypeStruct((n, nc, LANES), BF16),
        compiler_params=_cparams(("parallel",)),
        name="zero_slots",
    )()


def _dispatch_kernel(idx_ref, h3_ref, xs_in_hbm, xs_hbm, sem):
    del xs_in_hbm
    t = pl.program_id(0)
    tm = h3_ref.shape[0]
    for r in range(2 * tm):
        _row_copy(h3_ref, r % tm, xs_hbm, idx_ref[t, r], sem.at[0]).start(priority=r % 2)
    for r in range(2 * tm):
        _row_copy(h3_ref, r % tm, xs_hbm, 0, sem.at[0]).wait()


def _dispatch(h3, tok_slots, xs, *, tm):
    t, nc, _ = h3.shape
    grid_spec = pltpu.PrefetchScalarGridSpec(
        num_scalar_prefetch=1,
        grid=(t // tm,),
        in_specs=[pl.BlockSpec((tm, nc, LANES), lambda i, ix: (i, 0, 0)),
                  pl.BlockSpec(memory_space=pl.ANY)],
        out_specs=pl.BlockSpec(memory_space=pl.ANY),
        scratch_shapes=[pltpu.SemaphoreType.DMA((1,))],
    )
    return pl.pallas_call(
        _dispatch_kernel,
        grid_spec=grid_spec,
        out_shape=jax.ShapeDtypeStruct(xs.shape, xs.dtype),
        input_output_aliases={2: 0},
        compiler_params=_cparams(("arbitrary",)),
        name="dispatch",
    )(tok_slots, h3, xs)


def _moe_ffn_kernel(te_ref, nt_ref, x_ref, w1_ref, w3_ref, w2_ref, y_ref):
    t = pl.program_id(0)
    n_tiles = nt_ref[0]

    @pl.when(t < n_tiles)
    def _():
        x = _tiles_to_rows(x_ref[...])
        a = jnp.dot(x, w1_ref[...].astype(BF16), preferred_element_type=F32)
        c = jnp.dot(x, w3_ref[...].astype(BF16), preferred_element_type=F32)
        hid = (a * jax.nn.sigmoid(a)) * c
        y = jnp.dot(hid.astype(BF16), w2_ref[...].astype(BF16), preferred_element_type=F32)
        y_ref[...] = _rows_to_tiles(y.astype(BF16))

    @pl.when(t >= n_tiles)
    def _():
        y_ref[...] = jnp.zeros_like(y_ref)


def _moe_ffn(xs, tile_expert, n_tiles, w1, w3, w2):
    nc = xs.shape[1]
    nt = xs.shape[0] // MOE_TILE
    d = nc * LANES
    ff = w1.shape[2]
    grid_spec = pltpu.PrefetchScalarGridSpec(
        num_scalar_prefetch=2,
        grid=(nt,),
        in_specs=[
            pl.BlockSpec((MOE_TILE, nc, LANES), lambda t, te, nv: (jnp.minimum(t, nv[0] - 1), 0, 0)),
            pl.BlockSpec((None, d, ff), lambda t, te, nv: (te[t], 0, 0)),
            pl.BlockSpec((None, d, ff), lambda t, te, nv: (te[t], 0, 0)),
            pl.BlockSpec((None, ff, d), lambda t, te, nv: (te[t], 0, 0)),
        ],
        out_specs=pl.BlockSpec((MOE_TILE, nc, LANES), lambda t, te, nv: (t, 0, 0)),
    )
    return pl.pallas_call(
        _moe_ffn_kernel,
        grid_spec=grid_spec,
        out_shape=jax.ShapeDtypeStruct((nt * MOE_TILE, nc, LANES), BF16),
        compiler_params=_cparams(("arbitrary",)),
        name="moe_ffn",
    )(tile_expert, n_tiles, xs, w1, w3, w2)


def _combine_ln_kernel(idx_ref, h_ref, route_ref, g_ref, b_ref, y_hbm, o_ref, ybuf, sem, *, alpha):
    t = pl.program_id(0)
    nt = pl.num_programs(0)
    slot = t % 2
    rows = ybuf.shape[1]
    nxt = jnp.minimum(t + 1, nt - 1)

    @pl.when(t == 0)
    def _():
        _gather_rows(y_hbm, idx_ref, 0, rows, ybuf.at[0], sem.at[0], wait=False)

    _gather_rows(y_hbm, idx_ref, t, rows, ybuf.at[slot], sem.at[slot], wait=True)
    tm = rows // 2
    ya = _tiles_to_rows(ybuf[slot, pl.ds(0, tm)]).astype(F32)
    yb = _tiles_to_rows(ybuf[slot, pl.ds(tm, tm)]).astype(F32)
    _gather_rows(y_hbm, idx_ref, nxt, rows, ybuf.at[1 - slot], sem.at[1 - slot], wait=False)
    route = route_ref[...]
    moe = route[:, 2:3] * ya + route[:, 3:4] * yb
    o_ref[...] = _layer_norm(alpha * h_ref[...] + moe, g_ref[...], b_ref[...])

    @pl.when(t == nt - 1)
    def _():
        _gather_rows(y_hbm, idx_ref, nxt, rows, ybuf.at[1 - slot], sem.at[1 - slot], wait=True)


def _combine_ln(h, n_rows, tok_slots, route, y_sorted, ln_g, ln_b, *, alpha, tm):
    d = h.shape[1]
    nc = y_sorted.shape[1]
    nt = n_rows // tm
    const = lambda t, ix: (0, 0)
    row = lambda t, ix: (t, 0)
    grid_spec = pltpu.PrefetchScalarGridSpec(
        num_scalar_prefetch=1,
        grid=(nt,),
        in_specs=[pl.BlockSpec((tm, d), row),
                  pl.BlockSpec((tm, ROUTE_LANES), row),
                  pl.BlockSpec((1, d), const), pl.BlockSpec((1, d), const),
                  pl.BlockSpec(memory_space=pl.ANY)],
        out_specs=pl.BlockSpec((tm, d), row),
        scratch_shapes=[pltpu.VMEM((2, 2 * tm, nc, LANES), BF16), pltpu.SemaphoreType.DMA((2,))],
    )
    return pl.pallas_call(
        functools.partial(_combine_ln_kernel, alpha=alpha),
        grid_spec=grid_spec,
        out_shape=jax.ShapeDtypeStruct((n_rows, d), F32),
        compiler_params=_cparams(("arbitrary",)),
        name="combine_ln",
    )(tok_slots, h, route, ln_g.reshape(1, d), ln_b.reshape(1, d), y_sorted)


def _moe_plan(route, n_tokens):
    e = jnp.concatenate([route[:, 0], route[:, 1]]).astype(jnp.int32)
    n_assign = 2 * n_tokens
    nt = (n_assign + N_EXPERTS * (MOE_TILE - 1)) // MOE_TILE
    onehot = (e[:, None] == jnp.arange(N_EXPERTS, dtype=jnp.int32)[None, :]).astype(jnp.int32)
    csum = jnp.cumsum(onehot, axis=0)
    counts = csum[-1]
    rank = jnp.take_along_axis(csum, e[:, None], axis=1)[:, 0] - 1
    tiles_per_e = (counts + MOE_TILE - 1) // MOE_TILE
    tile_end = jnp.cumsum(tiles_per_e)
    tile_start = tile_end - tiles_per_e
    n_tiles = tile_end[-1:]
    slot = tile_start[e] * MOE_TILE + rank
    tile_ids = jnp.minimum(jnp.arange(nt, dtype=jnp.int32), n_tiles[0] - 1)
    tile_expert = jnp.minimum(jnp.sum((tile_end[None, :] <= tile_ids[:, None]).astype(jnp.int32), axis=1),
                              N_EXPERTS - 1)
    return tile_expert, n_tiles.astype(jnp.int32), nt * MOE_TILE, slot[:n_tokens], slot[n_tokens:]


def _tile_slots(slot_a, slot_b, tm):
    n = slot_a.shape[0] // tm
    return jnp.concatenate([slot_a.reshape(n, tm), slot_b.reshape(n, tm)], axis=1)


def kernel(x_prompt, x_sample, cache_k, cache_v, state_ret, page_table, w_in, sb_bias, w_branch_a, w_branch_b,
           w_out, gn_g, ln1_g, ln1_b, ln2_g, ln2_b, w_group_router, b_group_router, w_expert_router,
           b_expert_router, w1, w3, w2):
    depth = w_in.shape[0]
    assert depth == 1, "single-layer step"
    batch, seq, d = x_prompt.shape
    db, tn, _ = x_sample.shape
    sb_w = SB_HEADS * SB_HEAD_DIM
    qk_w = RET_HEADS * RET_DK
    v_w = RET_HEADS * RET_DV
    widths = (sb_w, sb_w, sb_w, qk_w, qk_w, v_w, v_w, d, d)
    offs = np.concatenate([[0], np.cumsum(widths)])
    o_q, o_k, o_rq, o_rv, o_ga, o_end = offs[0], offs[1], offs[3], offs[5], offs[7], offs[9]
    alpha = float((2.0 * depth) ** 0.25)
    sb_scale = float(SB_HEAD_DIM ** -0.5) * LOG2_E
    n_pages = page_table.shape[1]
    page = cache_k.shape[2]
    past = n_pages * page

    w_in0 = w_in[0]
    wa, wb, wo = w_branch_a[0], w_branch_b[0], w_out[0]
    bias = sb_bias[0] * LOG2_E
    gn = gn_g[0]
    w_route = jnp.concatenate(
        [w_group_router[0], w_expert_router[0].reshape(d, N_EXPERTS),
         jnp.zeros((d, ROUTE_LANES - N_GROUPS - N_EXPERTS), F32)], axis=1)
    b_route = jnp.concatenate(
        [b_group_router[0], b_expert_router[0].reshape(N_EXPERTS),
         jnp.zeros((ROUTE_LANES - N_GROUPS - N_EXPERTS,), F32)]).reshape(1, ROUTE_LANES)

    t_p = batch * seq
    t_s = db * tn
    t_all = t_p + t_s

    xp = x_prompt.reshape(t_p, d)
    xp16 = xp.astype(BF16)
    (q16,) = _mm(xp16, w_in0, o_q, sb_w, (BF16,), scale=sb_scale)
    k32, k16 = _mm(xp16, w_in0, o_k, sb_w, (F32, BF16))
    v32, v16 = _mm(xp16, w_in0, o_k + sb_w, sb_w, (F32, BF16))
    (rqk,) = _mm(xp16, w_in0, o_rq, 2 * qk_w, (F32,))
    (rvg,) = _mm(xp16, w_in0, o_rv, 2 * v_w, (BF16,))
    (gab,) = _mm(xp16, w_in0, o_ga, 2 * d, (BF16,))

    o_sb = _sb_prompt(q16, k16, v16, bias, batch, seq)

    n_chunks = seq // RET_CHUNK
    tables_p = _ret_tables(np.arange(seq), RET_CHUNK, RET_CHUNK)
    s0_p = jnp.zeros((batch, RET_HEADS, RET_DK, RET_DV), F32)
    ret_p, st_p = _retention((rqk, 0), (rqk, 1), (rvg, 0), (rvg, 1), gn, s0_p, batch, n_chunks, tables_p,
                             hp=False, out_dtype=BF16)
    m_p = _merge(o_sb, ret_p, wa, wb, gab, hp=False, out_dtype=BF16)
    tm_h = 256
    assert t_s <= tm_h
    h_p, h3_p, route_p = _out_ln(m_p, wo, xp, ln1_g[0], ln1_b[0], w_route, b_route,
                                 hp=False, alpha=alpha, tm=tm_h)

    xs = x_sample.reshape(t_s, d)
    (p_s,) = _mm(xs, w_in0, 0, int(o_end), (F32,), hp=True, tn=512)
    k_s = p_s[:, o_k:o_k + sb_w]
    v_s = p_s[:, o_k + sb_w:o_rq]
    q_s = p_s[:, o_q:o_k] * sb_scale
    o_sb_s = _sb_sample(q_s.reshape(db, tn, sb_w), k_s.reshape(db, tn, sb_w), v_s.reshape(db, tn, sb_w),
                        cache_k[0], cache_v[0], page_table, bias)

    def pad_rows(a):
        return jnp.pad(a.reshape(db, tn, -1), ((0, 0), (0, RET_CHUNK - tn), (0, 0))).reshape(db * RET_CHUNK, -1)

    pos_s = past + np.arange(RET_CHUNK)
    tables_s = _ret_tables(pos_s, tn, RET_CHUNK)
    rqk_s = pad_rows(p_s[:, o_rq:o_rv])
    rvg_s = pad_rows(p_s[:, o_rv:o_ga])
    ret_s_pad, st_s = _retention((rqk_s, 0), (rqk_s, 1), (rvg_s, 0), (rvg_s, 1), gn, state_ret[0], db, 1,
                                 tables_s, hp=True, out_dtype=F32)
    ret_s = ret_s_pad.reshape(db, RET_CHUNK, v_w)[:, :tn].reshape(t_s, v_w)
    m_s = _merge(o_sb_s.reshape(t_s, sb_w), ret_s, wa, wb, p_s[:, o_ga:o_end], hp=True, out_dtype=F32)
    h_s, h3_s, route_s = _out_ln(m_s, wo, xs, ln1_g[0], ln1_b[0], w_route, b_route, hp=True, alpha=alpha)

    route = jnp.concatenate([route_p, route_s], axis=0)
    tile_expert, n_tiles, n_slots, slot_a, slot_b = _moe_plan(route, t_all)
    tm_p = 256
    slots_p = _tile_slots(slot_a[:t_p], slot_b[:t_p], tm_p)
    slots_s = _tile_slots(slot_a[t_p:], slot_b[t_p:], t_s)
    x_sorted = _zero_tiles(n_slots, d // LANES)
    x_sorted = _dispatch(h3_p, slots_p, x_sorted, tm=tm_p)
    x_sorted = _dispatch(h3_s, slots_s, x_sorted, tm=t_s)
    y_sorted = _moe_ffn(x_sorted, tile_expert, n_tiles, w1[0], w3[0], w2[0])
    y_p = _combine_ln(h_p, t_p, slots_p, route_p, y_sorted, ln2_g[0], ln2_b[0], alpha=alpha, tm=tm_p)
    y_s = _combine_ln(h_s, t_s, slots_s, route_s, y_sorted, ln2_g[0], ln2_b[0], alpha=alpha, tm=t_s)

    hs = (SB_HEADS, SB_HEAD_DIM)
    return (y_p.reshape(batch, seq, d), y_s.reshape(db, tn, d),
            k32.reshape(1, batch, seq, *hs), v32.reshape(1, batch, seq, *hs),
            st_p[None],
            k_s.reshape(1, db, tn, *hs), v_s.reshape(1, db, tn, *hs),
            st_s[None])
```
